```python
import jax, jax.numpy as jnp
from jax import lax
import numpy as np

D_MODEL = 4096
BATCH = 2
SEQ = 4096
DEPTH = 4
DEC_BATCH = 8
DEC_SEQ = 64
PAST_LEN = 1024

CHUNK = 64
MLP_CHUNK = 128
N_MIXERS = 2
N_A = (DEPTH + N_MIXERS - 1) // N_MIXERS
N_B = DEPTH // N_MIXERS
D_A = D_MODEL
G_A = 8
D_B = D_MODEL
CONV_B = 31
D_FF = 2 * D_MODEL
CONV_F = 3
EPS = 1e-6

kernel_name = 'hybrid_gmlp_conformer_convffn_stream_step'


def rms_norm(x, g):
    xf = x.astype(jnp.float32)
    y = xf * lax.rsqrt(jnp.mean(xf * xf, axis=-1, keepdims=True) + EPS)
    return (y * g.astype(jnp.float32)).astype(x.dtype)


def layer_norm(x, g, b):
    xf = x.astype(jnp.float32)
    mu = jnp.mean(xf, axis=-1, keepdims=True)
    var = jnp.mean(jnp.square(xf - mu), axis=-1, keepdims=True)
    y = (xf - mu) * lax.rsqrt(var + EPS)
    return (y * g.astype(jnp.float32) + b.astype(jnp.float32)).astype(x.dtype)


def causal_dwconv(x, hist, w, b):
    xp = jnp.concatenate([hist.astype(x.dtype), x], axis=1)
    y = lax.conv_general_dilated(xp, w[:, None, :].astype(x.dtype), window_strides=(1,), padding='VALID',
                                 dimension_numbers=('NWC', 'WIO', 'NWC'), feature_group_count=x.shape[-1])
    return y + b, xp[:, xp.shape[1] - (w.shape[0] - 1):]


def chunk_causal_mask():
    pos = jnp.arange(MLP_CHUNK)
    return (pos[None, :] // CHUNK) <= (pos[:, None] // CHUNK)


def gmlp_mixer(h, w_in, b_in, ln_g, ln_b, w_s, b_s, w_out):
    n, t, _ = h.shape
    z = jax.nn.gelu(h @ w_in + b_in, approximate=False)
    u, v = jnp.split(z, 2, axis=-1)
    v = layer_norm(v, ln_g, ln_b)
    n_chunks = -(-t // MLP_CHUNK)
    pad = n_chunks * MLP_CHUNK - t
    vp = jnp.pad(v, ((0, 0), (0, pad), (0, 0))).reshape(n, n_chunks, MLP_CHUNK, G_A, D_A // G_A)
    w_m = jnp.where(chunk_causal_mask()[None], w_s, jnp.zeros((), w_s.dtype))
    mixed = jnp.einsum('gij,bcjgd->bcigd', w_m, vp) + jnp.transpose(b_s)[None, None, :, :, None]
    mixed = mixed.reshape(n, n_chunks * MLP_CHUNK, D_A)[:, :t]
    return (u * mixed) @ w_out, v


def conformer_conv(h, hist, w_in, b_in, w_dw, b_dw, ln_g, ln_b, w_out):
    a, gate = jnp.split(h @ w_in + b_in, 2, axis=-1)
    g = a * jax.nn.sigmoid(gate)
    c, new_hist = causal_dwconv(g, hist, w_dw, b_dw)
    c = jax.nn.silu(layer_norm(c, ln_g, ln_b))
    return c @ w_out, new_hist


def conv_ffn(h, hist, w_up, w_dw, b_dw, w_down):
    up = h @ w_up
    c, new_hist = causal_dwconv(up, hist, w_dw, b_dw)
    gate, val = jnp.split(c, 2, axis=-1)
    return (jax.nn.silu(gate) * val) @ w_down, new_hist


def trunk(x, conv_hist, ffn_hist, p):
    conv_states, ffn_states, v_rows = [], [], []
    for i in range(DEPTH):
        j = i // N_MIXERS
        h = rms_norm(x, p['norm_mix_pre'][i])
        if i % N_MIXERS == 0:
            m, v = gmlp_mixer(h, p['a_w_in'][j], p['a_b_in'][j], p['a_ln_g'][j], p['a_ln_b'][j],
                              p['a_w_s'][j], p['a_b_s'][j], p['a_w_out'][j])
            v_rows.append(v)
        else:
            m, st = conformer_conv(h, conv_hist[j], p['b_w_in'][j], p['b_b_in'][j], p['b_w_dw'][j],
                                   p['b_b_dw'][j], p['b_ln_g'][j], p['b_ln_b'][j], p['b_w_out'][j])
            conv_states.append(st)
        x = x + rms_norm(m, p['norm_mix_post'][i])
        h = rms_norm(x, p['norm_ffn_pre'][i])
        f, st = conv_ffn(h, ffn_hist[i], p['f_w_up'][i], p['f_w_dw'][i], p['f_b_dw'][i], p['f_w_down'][i])
        ffn_states.append(st)
        x = x + rms_norm(f, p['norm_ffn_post'][i])
    return x, jnp.stack(conv_states), jnp.stack(ffn_states), jnp.stack(v_rows)


def setup_inputs(seed: int = 0) -> dict:
    key = jax.random.key(seed)
    ks = jax.random.split(key, 28)

    def nrm(k, shape, scale):
        return jax.random.normal(k, shape, jnp.float32) * scale

    return {
        'x_prompt': nrm(ks[0], (BATCH, SEQ, D_MODEL), 1.0),
        'x_sample': nrm(ks[1], (DEC_BATCH, DEC_SEQ, D_MODEL), 1.0),
        'state_conv': nrm(ks[2], (N_B, DEC_BATCH, CONV_B - 1, D_B), 0.5),
        'state_ffn': nrm(ks[3], (DEPTH, DEC_BATCH, CONV_F - 1, 2 * D_FF), 1.0),
        'norm_mix_pre': 1.0 + nrm(ks[4], (DEPTH, D_MODEL), 0.05),
        'norm_mix_post': 1.0 + nrm(ks[5], (DEPTH, D_MODEL), 0.05),
        'norm_ffn_pre': 1.0 + nrm(ks[6], (DEPTH, D_MODEL), 0.05),
        'norm_ffn_post': 1.0 + nrm(ks[7], (DEPTH, D_MODEL), 0.05),
        'a_w_in': nrm(ks[8], (N_A, D_MODEL, 2 * D_A), D_MODEL ** -0.5),
        'a_b_in': nrm(ks[9], (N_A, 2 * D_A), 0.02),
        'a_ln_g': 1.0 + nrm(ks[10], (N_A, D_A), 0.05),
        'a_ln_b': nrm(ks[11], (N_A, D_A), 0.02),
        'a_w_s': nrm(ks[12], (N_A, G_A, MLP_CHUNK, MLP_CHUNK), MLP_CHUNK ** -0.5),
        'a_b_s': 1.0 + nrm(ks[13], (N_A, G_A, MLP_CHUNK), 0.1),
        'a_w_out': nrm(ks[14], (N_A, D_A, D_MODEL), D_A ** -0.5),
        'b_w_in': nrm(ks[15], (N_B, D_MODEL, 2 * D_B), D_MODEL ** -0.5),
        'b_b_in': nrm(ks[16], (N_B, 2 * D_B), 0.02),
        'b_w_dw': nrm(ks[17], (N_B, CONV_B, D_B), CONV_B ** -0.5),
        'b_b_dw': nrm(ks[18], (N_B, D_B), 0.02),
        'b_ln_g': 1.0 + nrm(ks[19], (N_B, D_B), 0.05),
        'b_ln_b': nrm(ks[20], (N_B, D_B), 0.02),
        'b_w_out': nrm(ks[21], (N_B, D_B, D_MODEL), D_B ** -0.5),
        'f_w_up': nrm(ks[22], (DEPTH, D_MODEL, 2 * D_FF), D_MODEL ** -0.5),
        'f_w_dw': nrm(ks[23], (DEPTH, CONV_F, 2 * D_FF), CONV_F ** -0.5),
        'f_b_dw': nrm(ks[24], (DEPTH, 2 * D_FF), 0.02),
        'f_w_down': nrm(ks[25], (DEPTH, D_FF, D_MODEL), D_FF ** -0.5),
    }


def reference(x_prompt, x_sample, state_conv, state_ffn,
              norm_mix_pre, norm_mix_post, norm_ffn_pre, norm_ffn_post,
              a_w_in, a_b_in, a_ln_g, a_ln_b, a_w_s, a_b_s, a_w_out,
              b_w_in, b_b_in, b_w_dw, b_b_dw, b_ln_g, b_ln_b, b_w_out,
              f_w_up, f_w_dw, f_b_dw, f_w_down):
    p = dict(norm_mix_pre=norm_mix_pre, norm_mix_post=norm_mix_post,
             norm_ffn_pre=norm_ffn_pre, norm_ffn_post=norm_ffn_post,
             a_w_in=a_w_in, a_b_in=a_b_in, a_ln_g=a_ln_g, a_ln_b=a_ln_b,
             a_w_s=a_w_s, a_b_s=a_b_s, a_w_out=a_w_out,
             b_w_in=b_w_in, b_b_in=b_b_in, b_w_dw=b_w_dw, b_b_dw=b_b_dw,
             b_ln_g=b_ln_g, b_ln_b=b_ln_b, b_w_out=b_w_out,
             f_w_up=f_w_up, f_w_dw=f_w_dw, f_b_dw=f_b_dw, f_w_down=f_w_down)
    bsz = x_prompt.shape[0]
    conv_hist0 = jnp.zeros((N_B, bsz, CONV_B - 1, D_B), x_prompt.dtype)
    ffn_hist0 = jnp.zeros((DEPTH, bsz, CONV_F - 1, 2 * D_FF), x_prompt.dtype)
    y_prompt, new_conv_prompt, new_ffn_prompt, _v_prompt = trunk(x_prompt, conv_hist0, ffn_hist0, p)
    y_sample, new_conv_sample, new_ffn_sample, new_v_sample = trunk(x_sample, state_conv, state_ffn, p)
    return (y_prompt, y_sample, new_conv_prompt, new_ffn_prompt, new_conv_sample, new_ffn_sample, new_v_sample)
```

```python
import functools

import jax
import jax.numpy as jnp
from jax import lax
from jax.experimental import pallas as pl
from jax.experimental.pallas import tpu as pltpu

D_MODEL = 4096
DEPTH = 4
SEQ = 4096
BATCH = 2
DEC_BATCH = 8
DEC_SEQ = 64
CONV_B = 31
CONV_F = 3
G_A = 8
MLP_CHUNK = 128
EPS = 1e-6

TM = 512
NSTREAM = 8
TSTEP = TM // NSTREAM
TN = 512
NCB = D_MODEL // TN
N_PROMPT_TILES = BATCH * SEQ // TM
N_TILES = N_PROMPT_TILES + 1
TILES_PER_SEQ = SEQ // TM
ROWS = N_TILES * TM
HIST_B = (CONV_B - 1) * NSTREAM
HIST_F = (CONV_F - 1) * NSTREAM
VMEM_LIMIT = 56 * 1024 * 1024
NORM_ROWS = 32

F32 = jnp.float32
BF16 = jnp.bfloat16


def _params(sem):
    return pltpu.CompilerParams(dimension_semantics=sem, vmem_limit_bytes=VMEM_LIMIT)


def _dot(a, b):
    return jnp.dot(a, b, preferred_element_type=F32)


def _gelu(x):
    return 0.5 * x * (1.0 + lax.erf(x * 0.7071067811865476))


def _sublane_iota(shape):
    return lax.broadcasted_iota(jnp.int32, shape, 1)


def _stream_history(tail_cur, tail_prev):
    rows, n = tail_cur.shape
    cur = pltpu.roll(tail_cur.reshape(rows // NSTREAM, NSTREAM, n), 1, 1)
    prev = pltpu.roll(tail_prev.reshape(rows // NSTREAM, NSTREAM, n), 1, 1)
    hist = jnp.where(_sublane_iota(cur.shape) == 0, prev, cur)
    return hist.reshape(rows, n)


def _rmsnorm_prologue(x_ref, g_ref, h_ref):
    def body(r, carry):
        sl = pl.ds(pl.multiple_of(r * NORM_ROWS, NORM_ROWS), NORM_ROWS)
        ss = jnp.zeros((NORM_ROWS, 1), F32)
        for cb in range(NCB):
            x = x_ref[cb, sl, :]
            ss = ss + jnp.sum(x * x, axis=-1, keepdims=True)
        scale = lax.rsqrt(ss * (1.0 / D_MODEL) + EPS)
        for cb in range(NCB):
            h_ref[sl, cb * TN:(cb + 1) * TN] = (x_ref[cb, sl, :] * scale * g_ref[cb]).astype(BF16)
        return carry

    lax.fori_loop(0, TM // NORM_ROWS, body, 0)


def _gelu_kernel(x_ref, g_ref, wa_ref, wb_ref, ba_ref, bb_ref, u_ref, v_ref, h_ref):
    @pl.when(pl.program_id(1) == 0)
    def _():
        _rmsnorm_prologue(x_ref, g_ref, h_ref)

    h = h_ref[...]
    u_ref[...] = _gelu(_dot(h, wa_ref[...]) + ba_ref[...]).astype(u_ref.dtype)
    v_ref[...] = _gelu(_dot(h, wb_ref[...]) + bb_ref[...])


def _glu_kernel(x_ref, g_ref, wa_ref, wb_ref, ba_ref, bb_ref, o_ref, h_ref):
    @pl.when(pl.program_id(1) == 0)
    def _():
        _rmsnorm_prologue(x_ref, g_ref, h_ref)

    h = h_ref[...]
    a = _dot(h, wa_ref[...]) + ba_ref[...]
    gate = _dot(h, wb_ref[...]) + bb_ref[...]
    o_ref[...] = a * jax.nn.sigmoid(gate)


def _ffn_up_kernel(x_ref, g_ref, wa_ref, wb_ref, dwa_ref, dwb_ref, dba_ref, dbb_ref, sta_ref, stb_ref,
                   s_ref, taila_ref, tailb_ref, h_ref, xpa_ref, xpb_ref, carry_ref):
    i = pl.program_id(0)
    j = pl.program_id(1)

    @pl.when(j == 0)
    def _():
        _rmsnorm_prologue(x_ref, g_ref, h_ref)

    @pl.when(i % TILES_PER_SEQ == 0)
    def _():
        carry_ref[j] = jnp.zeros(carry_ref.shape[1:], F32)

    h = h_ref[...]
    halves = ((wa_ref, dwa_ref, dba_ref, sta_ref, taila_ref, xpa_ref, 0),
              (wb_ref, dwb_ref, dbb_ref, stb_ref, tailb_ref, xpb_ref, 1))
    conv = []
    for w_ref, dw_ref, db_ref, st_ref, tail_ref, xp_ref, half in halves:
        up = _dot(h, w_ref[...])
        tail = up[TM - HIST_F:, :]
        tail_ref[...] = tail
        xp_ref[HIST_F:, :] = up

        @pl.when(i < N_PROMPT_TILES)
        def _():
            xp_ref[:HIST_F, :] = _stream_history(tail, carry_ref[j, half])

        @pl.when(i == N_PROMPT_TILES)
        def _():
            xp_ref[:HIST_F, :] = st_ref[...]

        carry_ref[j, half] = tail
        c = db_ref[...] + dw_ref[CONV_F - 1:CONV_F, :] * up
        for k in range(CONV_F - 1):
            c = c + dw_ref[k:k + 1, :] * xp_ref[k * NSTREAM:k * NSTREAM + TM, :]
        conv.append(c)
    gate, val = conv
    s_ref[...] = (gate * jax.nn.sigmoid(gate) * val).astype(s_ref.dtype)


def _pair_specs(n_half_blocks):
    x_spec = pl.BlockSpec((NCB, TM, TN), lambda i, j: (0, i, 0))
    g_spec = pl.BlockSpec((NCB, 1, TN), lambda i, j: (0, 0, 0))
    wa_spec = pl.BlockSpec((D_MODEL, TN), lambda i, j: (0, j))
    wb_spec = pl.BlockSpec((D_MODEL, TN), lambda i, j: (0, j + n_half_blocks))
    return x_spec, g_spec, wa_spec, wb_spec


def _row_pair_specs(rows, n_half_blocks):
    a = pl.BlockSpec((rows, TN), lambda i, j: (0, j))
    b = pl.BlockSpec((rows, TN), lambda i, j: (0, j + n_half_blocks))
    return a, b


def _mm_in_gelu(x3, gain3, w, b):
    nh = w.shape[1] // 2 // TN
    x_spec, g_spec, wa_spec, wb_spec = _pair_specs(nh)
    ba_spec, bb_spec = _row_pair_specs(1, nh)
    tile = pl.BlockSpec((TM, TN), lambda i, j: (i, j))
    return pl.pallas_call(
        _gelu_kernel,
        grid=(N_TILES, nh),
        in_specs=[x_spec, g_spec, wa_spec, wb_spec, ba_spec, bb_spec],
        out_specs=[tile, tile],
        out_shape=[jax.ShapeDtypeStruct((ROWS, nh * TN), BF16), jax.ShapeDtypeStruct((ROWS, nh * TN), F32)],
        scratch_shapes=[pltpu.VMEM((TM, D_MODEL), BF16)],
        compiler_params=_params(("arbitrary", "arbitrary")),
        name="gmlp_in",
    )(x3, gain3, w, w, b, b)


def _mm_in_glu(x3, gain3, w, b):
    nh = w.shape[1] // 2 // TN
    x_spec, g_spec, wa_spec, wb_spec = _pair_specs(nh)
    ba_spec, bb_spec = _row_pair_specs(1, nh)
    tile = pl.BlockSpec((TM, TN), lambda i, j: (i, j))
    return pl.pallas_call(
        _glu_kernel,
        grid=(N_TILES, nh),
        in_specs=[x_spec, g_spec, wa_spec, wb_spec, ba_spec, bb_spec],
        out_specs=tile,
        out_shape=jax.ShapeDtypeStruct((ROWS, nh * TN), F32),
        scratch_shapes=[pltpu.VMEM((TM, D_MODEL), BF16)],
        compiler_params=_params(("arbitrary", "arbitrary")),
        name="conformer_in",
    )(x3, gain3, w, w, b, b)


def _mm_in_ffn(x3, gain3, w, w_dw, b_dw, state_tm):
    nh = w.shape[1] // 2 // TN
    x_spec, g_spec, wa_spec, wb_spec = _pair_specs(nh)
    dwa_spec, dwb_spec = _row_pair_specs(CONV_F, nh)
    dba_spec, dbb_spec = _row_pair_specs(1, nh)
    sta_spec, stb_spec = _row_pair_specs(HIST_F, nh)
    tile = pl.BlockSpec((TM, TN), lambda i, j: (i, j))
    tail = pl.BlockSpec((HIST_F, TN), lambda i, j: (i, j))
    tail_shape = jax.ShapeDtypeStruct((N_TILES * HIST_F, nh * TN), F32)
    return pl.pallas_call(
        _ffn_up_kernel,
        grid=(N_TILES, nh),
        in_specs=[x_spec, g_spec, wa_spec, wb_spec, dwa_spec, dwb_spec, dba_spec, dbb_spec, sta_spec, stb_spec],
        out_specs=[tile, tail, tail],
        out_shape=[jax.ShapeDtypeStruct((ROWS, nh * TN), BF16), tail_shape, tail_shape],
        scratch_shapes=[pltpu.VMEM((TM, D_MODEL), BF16),
                        pltpu.VMEM((HIST_F + TM, TN), F32),
                        pltpu.VMEM((HIST_F + TM, TN), F32),
                        pltpu.VMEM((nh, 2, HIST_F, TN), F32)],
        compiler_params=_params(("arbitrary", "arbitrary")),
        name="ffn_up",
    )(x3, gain3, w, w, w_dw, w_dw, b_dw, b_dw, state_tm, state_tm)


def _mm_out_kernel(ksplit, lhs_ref, w_ref, x_ref, g_ref, o_ref, xs_ref):
    kk = pl.program_id(1)
    j = pl.program_id(2)
    m = _dot(lhs_ref[...], w_ref[...])
    if ksplit == 1:
        o_ref[j] = m
    else:
        @pl.when(kk == 0)
        def _():
            o_ref[j] = m

        @pl.when(kk > 0)
        def _():
            o_ref[j] += m

    @pl.when(kk == ksplit - 1)
    def _():
        xs_ref[j] = x_ref[0]

    @pl.when((kk == ksplit - 1) & (j == NCB - 1))
    def _():
        def body(r, carry):
            sl = pl.ds(pl.multiple_of(r * NORM_ROWS, NORM_ROWS), NORM_ROWS)
            ss = jnp.zeros((NORM_ROWS, 1), F32)
            for cb in range(NCB):
                m_cb = o_ref[cb, sl, :]
                ss = ss + jnp.sum(m_cb * m_cb, axis=-1, keepdims=True)
            scale = lax.rsqrt(ss * (1.0 / D_MODEL) + EPS)
            for cb in range(NCB):
                o_ref[cb, sl, :] = xs_ref[cb, sl, :] + o_ref[cb, sl, :] * scale * g_ref[cb]
            return carry

        lax.fori_loop(0, TM // NORM_ROWS, body, 0)


def _mm_out(lhs, w, x3, gain3, name):
    k = w.shape[0]
    ksplit = k // D_MODEL
    last = ksplit - 1
    return pl.pallas_call(
        functools.partial(_mm_out_kernel, ksplit),
        grid=(N_TILES, ksplit, NCB),
        in_specs=[pl.BlockSpec((TM, D_MODEL), lambda i, kk, j: (i, kk)),
                  pl.BlockSpec((D_MODEL, TN), lambda i, kk, j: (kk, j)),
                  pl.BlockSpec((1, TM, TN), lambda i, kk, j: (jnp.where(kk == last, j, 0), i, 0)),
                  pl.BlockSpec((NCB, 1, TN), lambda i, kk, j: (0, 0, 0))],
        out_specs=pl.BlockSpec((NCB, TM, TN), lambda i, kk, j: (0, i, 0)),
        out_shape=jax.ShapeDtypeStruct((NCB, ROWS, TN), F32),
        scratch_shapes=[pltpu.VMEM((NCB, TM, TN), F32)],
        compiler_params=_params(("arbitrary", "arbitrary", "arbitrary")),
        name=name,
    )(lhs, w, x3, gain3)


def _row_stats(src_ref, sum_ref, mu_ref, rstd_ref):
    mu = sum_ref[...] * (1.0 / D_MODEL)
    var = jnp.zeros((TM, 1), F32)
    for cb in range(NCB):
        d = src_ref[cb] - mu
        var = var + jnp.sum(d * d, axis=-1, keepdims=True)
    mu_ref[...] = mu
    rstd_ref[...] = lax.rsqrt(var * (1.0 / D_MODEL) + EPS)


def _accumulate_row_sum(c, sum_ref, val):
    @pl.when(c == 0)
    def _():
        sum_ref[...] = jnp.sum(val, axis=-1, keepdims=True)

    @pl.when(c > 0)
    def _():
        sum_ref[...] += jnp.sum(val, axis=-1, keepdims=True)


def _gmlp_mix_kernel(v_ref, u_ref, lng_ref, lnb_ref, wmix_ref, bmix_ref, p_ref, vn_ref,
                     v_scr, sum_ref, mu_ref, rstd_ref):
    ph = pl.program_id(1)
    c = pl.program_id(2)

    @pl.when(ph == 0)
    def _():
        v = v_ref[...]
        v_scr[c] = v
        _accumulate_row_sum(c, sum_ref, v)

    @pl.when((ph == 1) & (c == 0))
    def _():
        _row_stats(v_scr, sum_ref, mu_ref, rstd_ref)

    @pl.when(ph == 1)
    def _():
        vn = (v_scr[c] - mu_ref[...]) * rstd_ref[...] * lng_ref[c] + lnb_ref[c]
        vn_ref[...] = vn
        mixed = _dot(wmix_ref[0, 0], vn.astype(BF16)) + bmix_ref[0, 0]
        p_ref[...] = (u_ref[...].astype(F32) * mixed).astype(p_ref.dtype)


def _gmlp_mix(u, v, ln_g3, ln_b3, wmix, bmix):
    def variant(i):
        return jnp.where(i == N_PROMPT_TILES, 1, 0)

    return pl.pallas_call(
        _gmlp_mix_kernel,
        grid=(N_TILES, 2, NCB),
        in_specs=[pl.BlockSpec((TM, TN), lambda i, ph, c: (i, jnp.where(ph == 0, c, NCB - 1))),
                  pl.BlockSpec((TM, TN), lambda i, ph, c: (i, c * ph)),
                  pl.BlockSpec((NCB, 1, TN), lambda i, ph, c: (0, 0, 0)),
                  pl.BlockSpec((NCB, 1, TN), lambda i, ph, c: (0, 0, 0)),
                  pl.BlockSpec((1, 1, TM, TM), lambda i, ph, c: (variant(i), c * ph, 0, 0)),
                  pl.BlockSpec((1, 1, TM, 1), lambda i, ph, c: (variant(i), c * ph, 0, 0))],
        out_specs=[pl.BlockSpec((TM, TN), lambda i, ph, c: (i, c * ph)),
                   pl.BlockSpec((TM, TN), lambda i, ph, c: (i, c * ph))],
        out_shape=[jax.ShapeDtypeStruct((ROWS, D_MODEL), BF16), jax.ShapeDtypeStruct((ROWS, D_MODEL), F32)],
        scratch_shapes=[pltpu.VMEM((NCB, TM, TN), F32), pltpu.VMEM((TM, 1), F32),
                        pltpu.VMEM((TM, 1), F32), pltpu.VMEM((TM, 1), F32)],
        compiler_params=_params(("arbitrary", "arbitrary", "arbitrary")),
        name="gmlp_mix",
    )(v, u, ln_g3, ln_b3, wmix, bmix)


CONV_ROWS = 128
CONV_LANES = 128


def _conf_conv_kernel(g_ref, prev_ref, st_ref, dw_ref, db_ref, lng_ref, lnb_ref, q_ref,
                      xp_ref, c_scr, sum_ref, mu_ref, rstd_ref):
    i = pl.program_id(0)
    ph = pl.program_id(1)
    c = pl.program_id(2)

    @pl.when(ph == 0)
    def _():
        xp_ref[HIST_B:, :] = g_ref[...]
        tail = g_ref[TM - HIST_B:, :]

        @pl.when(i == N_PROMPT_TILES)
        def _():
            xp_ref[:HIST_B, :] = st_ref[...]

        @pl.when((i < N_PROMPT_TILES) & (i % TILES_PER_SEQ == 0))
        def _():
            xp_ref[:HIST_B, :] = _stream_history(tail, jnp.zeros_like(tail))

        @pl.when((i < N_PROMPT_TILES) & (i % TILES_PER_SEQ != 0))
        def _():
            xp_ref[:HIST_B, :] = _stream_history(tail, prev_ref[prev_ref.shape[0] - HIST_B:, :])

        def body(r, carry):
            row0 = pl.multiple_of(r * CONV_ROWS, CONV_ROWS)
            for lc in range(TN // CONV_LANES):
                lanes = slice(lc * CONV_LANES, (lc + 1) * CONV_LANES)
                acc = jnp.broadcast_to(db_ref[:, lanes], (CONV_ROWS, CONV_LANES))
                for k in range(CONV_B):
                    acc = acc + dw_ref[k:k + 1, lanes] * xp_ref[pl.ds(row0 + k * NSTREAM, CONV_ROWS), lanes]
                c_scr[c, pl.ds(row0, CONV_ROWS), lanes] = acc
            return carry

        lax.fori_loop(0, TM // CONV_ROWS, body, 0)
        _accumulate_row_sum(c, sum_ref, c_scr[c])

    @pl.when((ph == 1) & (c == 0))
    def _():
        _row_stats(c_scr, sum_ref, mu_ref, rstd_ref)

    @pl.when(ph == 1)
    def _():
        y = (c_scr[c] - mu_ref[...]) * rstd_ref[...] * lng_ref[c] + lnb_ref[c]
        q_ref[...] = (y * jax.nn.sigmoid(y)).astype(q_ref.dtype)


def _conf_conv(g, state_tm, w_dw, b_dw, ln_g3, ln_b3):
    prev_rows = TM // 2

    def col(ph, c):
        return jnp.where(ph == 0, c, NCB - 1)

    return pl.pallas_call(
        _conf_conv_kernel,
        grid=(N_TILES, 2, NCB),
        in_specs=[pl.BlockSpec((TM, TN), lambda i, ph, c: (i, col(ph, c))),
                  pl.BlockSpec((prev_rows, TN), lambda i, ph, c: (jnp.maximum(2 * i - 1, 0), col(ph, c))),
                  pl.BlockSpec((HIST_B, TN), lambda i, ph, c: (0, col(ph, c))),
                  pl.BlockSpec((CONV_B, TN), lambda i, ph, c: (0, col(ph, c))),
                  pl.BlockSpec((1, TN), lambda i, ph, c: (0, col(ph, c))),
                  pl.BlockSpec((NCB, 1, TN), lambda i, ph, c: (0, 0, 0)),
                  pl.BlockSpec((NCB, 1, TN), lambda i, ph, c: (0, 0, 0))],
        out_specs=pl.BlockSpec((TM, TN), lambda i, ph, c: (i, c * ph)),
        out_shape=jax.ShapeDtypeStruct((ROWS, D_MODEL), BF16),
        scratch_shapes=[pltpu.VMEM((HIST_B + TM, TN), F32), pltpu.VMEM((NCB, TM, TN), F32),
                        pltpu.VMEM((TM, 1), F32), pltpu.VMEM((TM, 1), F32), pltpu.VMEM((TM, 1), F32)],
        compiler_params=_params(("arbitrary", "arbitrary", "arbitrary")),
        name="conformer_conv",
    )(g, g, state_tm, w_dw, b_dw, ln_g3, ln_b3)


def _blocked(vec):
    return vec.reshape(NCB, 1, TN)


def _to_tiles(x_prompt, x_sample):
    p = x_prompt.reshape(BATCH * TILES_PER_SEQ, NSTREAM, TSTEP, NCB, TN)
    p = p.transpose(3, 0, 2, 1, 4).reshape(NCB, N_PROMPT_TILES * TM, TN)
    s = x_sample.reshape(DEC_BATCH, DEC_SEQ, NCB, TN).transpose(2, 1, 0, 3).reshape(NCB, TM, TN)
    return jnp.concatenate([p, s], axis=1)


def _rows_to_natural(rows, width):
    p = rows[:N_PROMPT_TILES * TM].reshape(BATCH * TILES_PER_SEQ, TSTEP, NSTREAM, width)
    p = p.transpose(0, 2, 1, 3).reshape(BATCH, SEQ, width)
    s = rows[N_PROMPT_TILES * TM:].reshape(DEC_SEQ, DEC_BATCH, width).transpose(1, 0, 2)
    return p, s


def _from_tiles(x3):
    rows = x3.transpose(1, 0, 2).reshape(ROWS, D_MODEL)
    return _rows_to_natural(rows, D_MODEL)


def _state_time_major(state):
    return state.transpose(1, 0, 2).reshape(state.shape[1] * DEC_BATCH, state.shape[2])


def _mix_weights(w_s, b_s):
    pos = jnp.arange(MLP_CHUNK)
    half = MLP_CHUNK // 2
    mask = (pos[None, :] // half) <= (pos[:, None] // half)
    w_m = jnp.where(mask[None], w_s, jnp.zeros((), w_s.dtype))
    eye_p = jnp.eye(TM // MLP_CHUNK, dtype=w_s.dtype)
    eye_s = jnp.eye(DEC_BATCH, dtype=w_s.dtype)
    nat_p = jnp.einsum("ab,gij->gaibj", eye_p, w_m).reshape(G_A, TM, TM)
    nat_s = jnp.einsum("ab,gij->gaibj", eye_s, w_m[:, :DEC_SEQ, :DEC_SEQ]).reshape(G_A, TM, TM)
    r = jnp.arange(TM)
    perm = (r % NSTREAM) * TSTEP + r // NSTREAM
    wmix = jnp.stack([nat_p, nat_s])[:, :, perm][:, :, :, perm].astype(BF16)
    b_p = b_s[:, perm % MLP_CHUNK]
    b_smp = b_s[:, perm % TSTEP]
    bmix = jnp.stack([b_p, b_smp])[..., None]
    return wmix, bmix


def kernel(x_prompt, x_sample, state_conv, state_ffn, norm_mix_pre, norm_mix_post, norm_ffn_pre, norm_ffn_post,
           a_w_in, a_b_in, a_ln_g, a_ln_b, a_w_s, a_b_s, a_w_out,
           b_w_in, b_b_in, b_w_dw, b_b_dw, b_ln_g, b_ln_b, b_w_out,
           f_w_up, f_w_dw, f_b_dw, f_w_down):
    x3 = _to_tiles(x_prompt, x_sample)
    last_prompt_tiles = [TILES_PER_SEQ * (b + 1) - 1 for b in range(BATCH)]

    conv_p, conv_s, ffn_p, ffn_s, v_s = [], [], [], [], []
    for layer in range(DEPTH):
        j = layer // 2
        pre = _blocked(norm_mix_pre[layer])
        post = _blocked(norm_mix_post[layer])
        if layer % 2 == 0:
            u, v = _mm_in_gelu(x3, pre, a_w_in[j].astype(BF16), a_b_in[j][None])
            wmix, bmix = _mix_weights(a_w_s[j], a_b_s[j])
            p, vn = _gmlp_mix(u, v, _blocked(a_ln_g[j]), _blocked(a_ln_b[j]), wmix, bmix)
            v_s.append(_rows_to_natural(vn, D_MODEL)[1])
            x3 = _mm_out(p, a_w_out[j].astype(BF16), x3, post, "gmlp_out")
        else:
            g = _mm_in_glu(x3, pre, b_w_in[j].astype(BF16), b_b_in[j][None])
            q = _conf_conv(g, _state_time_major(state_conv[j]), b_w_dw[j], b_b_dw[j][None],
                           _blocked(b_ln_g[j]), _blocked(b_ln_b[j]))
            g_p, g_s = _rows_to_natural(g, D_MODEL)
            conv_p.append(g_p[:, SEQ - (CONV_B - 1):])
            conv_s.append(g_s[:, DEC_SEQ - (CONV_B - 1):])
            x3 = _mm_out(q, b_w_out[j].astype(BF16), x3, post, "conformer_out")

        s, tail_a, tail_b = _mm_in_ffn(x3, _blocked(norm_ffn_pre[layer]), f_w_up[layer].astype(BF16),
                                       f_w_dw[layer], f_b_dw[layer][None], _state_time_major(state_ffn[layer]))
        tails = jnp.concatenate([tail_a, tail_b], axis=-1).reshape(N_TILES, CONV_F - 1, NSTREAM, -1)
        ffn_p.append(jnp.stack([tails[t, :, NSTREAM - 1] for t in last_prompt_tiles]))
        ffn_s.append(tails[N_PROMPT_TILES].transpose(1, 0, 2))
        x3 = _mm_out(s, f_w_down[layer].astype(BF16), x3, _blocked(norm_ffn_post[layer]), "ffn_down")

    y_prompt, y_sample = _from_tiles(x3)
    return (y_prompt, y_sample, jnp.stack(conv_p), jnp.stack(ffn_p), jnp.stack(conv_s), jnp.stack(ffn_s),
            jnp.stack(v_s))
```

```python
import functools

import jax
import jax.numpy as jnp
from jax import lax
from jax.experimental import pallas as pl
from jax.experimental.pallas import tpu as pltpu

D_MODEL = 4096
DEPTH = 4
SEQ = 4096
BATCH = 2
DEC_BATCH = 8
DEC_SEQ = 64
CONV_B = 31
CONV_F = 3
G_A = 8
MLP_CHUNK = 128
EPS = 1e-6

TM = 512
NSTREAM = 8
TSTEP = TM // NSTREAM
TN = 512
NCB = D_MODEL // TN
N_PROMPT_TILES = BATCH * SEQ // TM
N_TILES = N_PROMPT_TILES + 1
TILES_PER_SEQ = SEQ // TM
ROWS = N_TILES * TM
HIST_B = (CONV_B - 1) * NSTREAM
HIST_F = (CONV_F - 1) * NSTREAM
VMEM_LIMIT = 56 * 1024 * 1024
NORM_ROWS = 32

F32 = jnp.float32
BF16 = jnp.bfloat16


def _params(sem):
    return pltpu.CompilerParams(dimension_semantics=sem, vmem_limit_bytes=VMEM_LIMIT)


def _dot(a, b):
    return jnp.dot(a, b, preferred_element_type=F32)


def _gelu(x):
    return 0.5 * x * (1.0 + lax.erf(x * 0.7071067811865476))


def _silu(x):
    return x * jax.nn.sigmoid(x)


def _sublane_iota(shape):
    return lax.broadcasted_iota(jnp.int32, shape, 1)


def _stream_history(tail_cur, tail_prev):
    rows, n = tail_cur.shape
    cur = pltpu.roll(tail_cur.reshape(rows // NSTREAM, NSTREAM, n), 1, 1)
    prev = pltpu.roll(tail_prev.reshape(rows // NSTREAM, NSTREAM, n), 1, 1)
    hist = jnp.where(_sublane_iota(cur.shape) == 0, prev, cur)
    return hist.reshape(rows, n)


def _rms_first_kernel(x_ref, g_ref, h_ref):
    def body(r, carry):
        sl = pl.ds(pl.multiple_of(r * NORM_ROWS, NORM_ROWS), NORM_ROWS)
        ss = jnp.zeros((NORM_ROWS, 1), F32)
        for cb in range(NCB):
            x = x_ref[cb, sl, :]
            ss = ss + jnp.sum(x * x, axis=-1, keepdims=True)
        scale = lax.rsqrt(ss * (1.0 / D_MODEL) + EPS)
        for cb in range(NCB):
            h_ref[sl, cb * TN:(cb + 1) * TN] = (x_ref[cb, sl, :] * scale * g_ref[cb]).astype(BF16)
        return carry

    lax.fori_loop(0, TM // NORM_ROWS, body, 0)


def _rms_first(x3, gain3):
    return pl.pallas_call(
        _rms_first_kernel,
        grid=(N_TILES,),
        in_specs=[pl.BlockSpec((NCB, TM, TN), lambda i: (0, i, 0)),
                  pl.BlockSpec((NCB, 1, TN), lambda i: (0, 0, 0))],
        out_specs=pl.BlockSpec((TM, D_MODEL), lambda i: (i, 0)),
        out_shape=jax.ShapeDtypeStruct((ROWS, D_MODEL), BF16),
        compiler_params=_params(("arbitrary",)),
        name="rms_first",
    )(x3, gain3)


def _gelu_epilogue(rows, prev, raw, ba_ref, bb_ref, u_ref, v_ref):
    u_ref[rows, :] = _gelu(raw[0, rows, :] + ba_ref[...]).astype(u_ref.dtype)
    v_ref[rows, :] = _gelu(raw[1, rows, :] + bb_ref[...])


def _glu_epilogue(rows, prev, raw, ba_ref, bb_ref, o_ref):
    o_ref[rows, :] = (raw[0, rows, :] + ba_ref[...]) * jax.nn.sigmoid(raw[1, rows, :] + bb_ref[...])


def _ffn_history(prev, raw, dwa_ref, dwb_ref, dba_ref, dbb_ref, sta_ref, stb_ref, s_ref, taila_ref, tailb_ref,
                 hist_ref, carry_ref):
    ip, jp = prev
    for half, (st_ref, tail_ref) in enumerate(((sta_ref, taila_ref), (stb_ref, tailb_ref))):
        tail = raw[half, TM - HIST_F:, :]
        tail_ref[...] = tail

        @pl.when(ip % TILES_PER_SEQ == 0)
        def _():
            carry_ref[jp, half] = jnp.zeros((HIST_F, TN), F32)

        @pl.when(ip < N_PROMPT_TILES)
        def _():
            hist_ref[half] = _stream_history(tail, carry_ref[jp, half])

        @pl.when(ip == N_PROMPT_TILES)
        def _():
            hist_ref[half] = st_ref[...]

        carry_ref[jp, half] = tail


def _ffn_epilogue(rows, prev, raw, dwa_ref, dwb_ref, dba_ref, dbb_ref, sta_ref, stb_ref, s_ref, taila_ref,
                  tailb_ref, hist_ref, carry_ref):
    for r0 in range(rows.start, rows.stop, EPI_ROWS):
        conv = []
        for half, (dw_ref, db_ref) in enumerate(((dwa_ref, dba_ref), (dwb_ref, dbb_ref))):
            c = db_ref[...] + dw_ref[CONV_F - 1:CONV_F, :] * raw[half, r0:r0 + EPI_ROWS, :]
            for k in range(CONV_F - 1):
                back = (CONV_F - 1 - k) * NSTREAM
                if r0 == 0:
                    window = jnp.concatenate([hist_ref[half, HIST_F - back:, :], raw[half, :EPI_ROWS - back, :]],
                                             axis=0)
                else:
                    window = raw[half, r0 - back:r0 + EPI_ROWS - back, :]
                c = c + dw_ref[k:k + 1, :] * window
            conv.append(c)
        gate, val = conv
        s_ref[r0:r0 + EPI_ROWS, :] = (_silu(gate) * val).astype(s_ref.dtype)


DOT_ROWS = 256
EPI_ROWS = 32


def _mm_in_kernel(nh, n_epi_in, n_out, pre_fn, epi_fn, h_ref, wa_ref, wb_ref, *rest):
    epi_refs = rest[:n_epi_in + n_out]
    raw_a, raw_b = rest[n_epi_in + n_out:n_epi_in + n_out + 2]
    extra = rest[n_epi_in + n_out + 2:]
    s = pl.program_id(0)
    t = jnp.maximum(s - 1, 0)
    prev = (t // nh, t % nh)

    @pl.when(s == 0)
    def _():
        raw_b[...] = jnp.zeros(raw_b.shape, F32)

    def step(raw_w, raw_r):
        if pre_fn is not None:
            pre_fn(prev, raw_r, *epi_refs, *extra)
        pieces = [(c, half) for c in range(TM // DOT_ROWS) for half in (0, 1)]
        chunk = TM // len(pieces)
        pending = None
        for k, (c, half) in enumerate(pieces):
            rows = slice(c * DOT_ROWS, (c + 1) * DOT_ROWS)
            r = _dot(h_ref[rows, :], (wa_ref, wb_ref)[half][...])
            if pending is not None:
                raw_w[pending[0], pending[1], :] = pending[2]
            pending = (half, rows, r)
            epi_fn(slice(k * chunk, (k + 1) * chunk), prev, raw_r, *epi_refs, *extra)
        raw_w[pending[0], pending[1], :] = pending[2]

    @pl.when(s % 2 == 0)
    def _():
        step(raw_a, raw_b)

    @pl.when(s % 2 == 1)
    def _():
        step(raw_b, raw_a)


def _mm_in(kind, h, w, layer, epi_inputs, name):
    nh = w.shape[2] // 2 // TN
    steps = N_TILES * nh

    def cur(s):
        t = jnp.minimum(s, steps - 1)
        return t // nh, t % nh

    def prev(s):
        t = jnp.maximum(s - 1, 0)
        return t // nh, t % nh

    def half_specs(rows):
        a = pl.BlockSpec((None, rows, TN), lambda s: (layer, 0, prev(s)[1]))
        b = pl.BlockSpec((None, rows, TN), lambda s: (layer, 0, prev(s)[1] + nh))
        return [a, b]

    tile = pl.BlockSpec((TM, TN), lambda s: prev(s))
    in_specs = [pl.BlockSpec((TM, D_MODEL), lambda s: (cur(s)[0], 0)),
                pl.BlockSpec((None, D_MODEL, TN), lambda s: (layer, 0, cur(s)[1])),
                pl.BlockSpec((None, D_MODEL, TN), lambda s: (layer, 0, cur(s)[1] + nh))]
    scratch = [pltpu.VMEM((2, TM, TN), F32), pltpu.VMEM((2, TM, TN), F32)]
    cols = nh * TN
    if kind == "gelu":
        (bias,) = epi_inputs
        args = [bias, bias]
        in_specs += half_specs(1)
        out_specs = [tile, tile]
        out_shape = [jax.ShapeDtypeStruct((ROWS, cols), BF16), jax.ShapeDtypeStruct((ROWS, cols), F32)]
        pre_fn, epi_fn = None, _gelu_epilogue
    elif kind == "glu":
        (bias,) = epi_inputs
        args = [bias, bias]
        in_specs += half_specs(1)
        out_specs = [tile]
        out_shape = [jax.ShapeDtypeStruct((ROWS, cols), F32)]
        pre_fn, epi_fn = None, _glu_epilogue
    else:
        w_dw, b_dw, state_tm = epi_inputs
        args = [w_dw, w_dw, b_dw, b_dw, state_tm, state_tm]
        in_specs += half_specs(CONV_F) + half_specs(1) + half_specs(HIST_F)
        tail = pl.BlockSpec((HIST_F, TN), lambda s: prev(s))
        tail_shape = jax.ShapeDtypeStruct((N_TILES * HIST_F, cols), F32)
        out_specs = [tile, tail, tail]
        out_shape = [jax.ShapeDtypeStruct((ROWS, cols), BF16), tail_shape, tail_shape]
        scratch += [pltpu.VMEM((2, HIST_F, TN), F32), pltpu.VMEM((nh, 2, HIST_F, TN), F32)]
        pre_fn, epi_fn = _ffn_history, _ffn_epilogue
    return pl.pallas_call(
        functools.partial(_mm_in_kernel, nh, len(args), len(out_specs), pre_fn, epi_fn),
        grid=(steps + 1,),
        in_specs=in_specs,
        out_specs=out_specs,
        out_shape=out_shape,
        scratch_shapes=scratch,
        compiler_params=_params(("arbitrary",)),
        name=name,
    )(h, w, w, *args)


OUT_TN = 1024
OUT_COL_STEPS = D_MODEL // OUT_TN
OUT_COL_BLOCKS = OUT_TN // TN


def _mm_out_kernel(kpass, emit_h, lhs_ref, w_ref, x_ref, gpost_ref, gpre_ref, *rest):
    if emit_h:
        xo_ref, h_ref, m_ref = rest
    else:
        xo_ref, m_ref = rest
        h_ref = None
    i = pl.program_id(0)
    kk = pl.program_id(1)
    j = pl.program_id(2)
    pieces = [(c, ch) for c in range(TM // DOT_ROWS) for ch in range(OUT_COL_BLOCKS)]
    rows = TM // (kpass * OUT_COL_STEPS)
    sub = rows // len(pieces)
    q = kk * OUT_COL_STEPS + j
    m_cur = m_ref.at[i % 2]
    m_prev = m_ref.at[(i + 1) % 2]

    def norm_rows(k):
        sl = pl.ds(pl.multiple_of(q * rows + k * sub, sub), sub)
        xs = slice(k * sub, (k + 1) * sub)
        ss = jnp.zeros((sub, 1), F32)
        for cb in range(NCB):
            m = m_prev[cb, sl, :]
            ss = ss + jnp.sum(m * m, axis=-1, keepdims=True)
        scale = lax.rsqrt(ss * (1.0 / D_MODEL) + EPS)
        ss = jnp.zeros((sub, 1), F32)
        for cb in range(NCB):
            xn = x_ref[cb, xs, :] + m_prev[cb, sl, :] * scale * gpost_ref[cb]
            xo_ref[cb, xs, :] = xn
            ss = ss + jnp.sum(xn * xn, axis=-1, keepdims=True)
        if emit_h:
            scale = lax.rsqrt(ss * (1.0 / D_MODEL) + EPS)
            for cb in range(NCB):
                h_ref[xs, cb * TN:(cb + 1) * TN] = (xo_ref[cb, xs, :] * scale * gpre_ref[cb]).astype(BF16)

    def body(do_dot, accumulate, do_norm):
        def flush(cb, rws, r):
            m_cur[cb, rws, :] = m_cur[cb, rws, :] + r if accumulate else r

        pending = None
        for k, (c, ch) in enumerate(pieces):
            if do_dot:
                rws = slice(c * DOT_ROWS, (c + 1) * DOT_ROWS)
                r = _dot(lhs_ref[rws, :], w_ref[:, ch * TN:(ch + 1) * TN])
                if pending is not None:
                    flush(*pending)
                pending = (j * OUT_COL_BLOCKS + ch, rws, r)
            if do_norm:
                norm_rows(k)
        if pending is not None:
            flush(*pending)

    for accumulate in ((False,) if kpass == 1 else (False, True)):
        k_cond = (kk > 0) if accumulate else (kk == 0)

        @pl.when((i == 0) & k_cond)
        def _():
            body(True, accumulate, False)

        @pl.when((i >= 1) & (i < N_TILES) & k_cond)
        def _():
            body(True, accumulate, True)

    @pl.when(i == N_TILES)
    def _():
        body(False, False, True)


def _mm_out(lhs, w, layer, x3, gpost3, gpre3, name):
    kpass = w.shape[1] // D_MODEL
    nq = kpass * OUT_COL_STEPS
    rows = TM // nq
    emit_h = gpre3 is not None
    if not emit_h:
        gpre3 = gpost3

    def live(i, v, last):
        return jnp.where(i < N_TILES, v, last)

    def chunk(i, kk, j):
        return jnp.where(i >= 1, (i - 1) * nq + kk * OUT_COL_STEPS + j, 0)

    xblock = pl.BlockSpec((NCB, rows, TN), lambda i, kk, j: (0, chunk(i, kk, j), 0))
    gain = pl.BlockSpec((NCB, 1, TN), lambda i, kk, j: (0, 0, 0))
    out_specs = [xblock]
    out_shape = [jax.ShapeDtypeStruct((NCB, ROWS, TN), F32)]
    if emit_h:
        out_specs.append(pl.BlockSpec((rows, D_MODEL), lambda i, kk, j: (chunk(i, kk, j), 0)))
        out_shape.append(jax.ShapeDtypeStruct((ROWS, D_MODEL), BF16))
    outs = pl.pallas_call(
        functools.partial(_mm_out_kernel, kpass, emit_h),
        grid=(N_TILES + 1, kpass, OUT_COL_STEPS),
        in_specs=[pl.BlockSpec((TM, D_MODEL), lambda i, kk, j: (jnp.minimum(i, N_TILES - 1), live(i, kk, kpass - 1))),
                  pl.BlockSpec((None, D_MODEL, OUT_TN),
                               lambda i, kk, j: (layer, live(i, kk, kpass - 1), live(i, j, OUT_COL_STEPS - 1))),
                  xblock, gain, gain],
        out_specs=out_specs,
        out_shape=out_shape,
        scratch_shapes=[pltpu.VMEM((2, NCB, TM, TN), F32)],
        compiler_params=_params(("arbitrary", "arbitrary", "arbitrary")),
        name=name,
    )(lhs, w, x3, gpost3, gpre3)
    return outs if emit_h else (outs[0], None)


def _row_stats(src_ref, sum_ref, mu_ref, rstd_ref):
    mu = sum_ref[...] * (1.0 / D_MODEL)
    var = jnp.zeros((TM, 1), F32)
    for cb in range(NCB):
        d = src_ref[cb] - mu
        var = var + jnp.sum(d * d, axis=-1, keepdims=True)
    mu_ref[...] = mu
    rstd_ref[...] = lax.rsqrt(var * (1.0 / D_MODEL) + EPS)


def _accumulate_row_sum(c, sum_ref, val):
    @pl.when(c == 0)
    def _():
        sum_ref[...] = jnp.sum(val, axis=-1, keepdims=True)

    @pl.when(c > 0)
    def _():
        sum_ref[...] += jnp.sum(val, axis=-1, keepdims=True)


def _gmlp_mix_kernel(v_ref, u_ref, lng_ref, lnb_ref, wmix_ref, bmix_ref, p_ref, vn_ref,
                     v_scr, sum_ref, mu_ref, rstd_ref):
    ph = pl.program_id(1)
    c = pl.program_id(2)

    @pl.when(ph == 0)
    def _():
        v = v_ref[...]
        v_scr[c] = v
        _accumulate_row_sum(c, sum_ref, v)

    @pl.when((ph == 1) & (c == 0))
    def _():
        _row_stats(v_scr, sum_ref, mu_ref, rstd_ref)

    @pl.when(ph == 1)
    def _():
        vn = (v_scr[c] - mu_ref[...]) * rstd_ref[...] * lng_ref[c] + lnb_ref[c]
        vn_ref[...] = vn
        mixed = _dot(wmix_ref[0, 0], vn.astype(BF16)) + bmix_ref[0, 0]
        p_ref[...] = (u_ref[...].astype(F32) * mixed).astype(p_ref.dtype)


def _gmlp_mix(u, v, ln_g3, ln_b3, wmix, bmix):
    def variant(i):
        return jnp.where(i == N_PROMPT_TILES, 1, 0)

    return pl.pallas_call(
        _gmlp_mix_kernel,
        grid=(N_TILES, 2, NCB),
        in_specs=[pl.BlockSpec((TM, TN), lambda i, ph, c: (i, jnp.where(ph == 0, c, NCB - 1))),
                  pl.BlockSpec((TM, TN), lambda i, ph, c: (i, c * ph)),
                  pl.BlockSpec((NCB, 1, TN), lambda i, ph, c: (0, 0, 0)),
                  pl.BlockSpec((NCB, 1, TN), lambda i, ph, c: (0, 0, 0)),
                  pl.BlockSpec((1, 1, TM, TM), lambda i, ph, c: (variant(i), c * ph, 0, 0)),
                  pl.BlockSpec((1, 1, TM, 1), lambda i, ph, c: (variant(i), c * ph, 0, 0))],
        out_specs=[pl.BlockSpec((TM, TN), lambda i, ph, c: (i, c * ph)),
                   pl.BlockSpec((TM, TN), lambda i, ph, c: (i, c * ph))],
        out_shape=[jax.ShapeDtypeStruct((ROWS, D_MODEL), BF16), jax.ShapeDtypeStruct((ROWS, D_MODEL), F32)],
        scratch_shapes=[pltpu.VMEM((NCB, TM, TN), F32), pltpu.VMEM((TM, 1), F32),
                        pltpu.VMEM((TM, 1), F32), pltpu.VMEM((TM, 1), F32)],
        compiler_params=_params(("arbitrary", "arbitrary", "arbitrary")),
        name="gmlp_mix",
    )(v, u, ln_g3, ln_b3, wmix, bmix)


CONV_ROWS = 128
CONV_LANES = 128


def _conf_conv_kernel(g_ref, prev_ref, st_ref, dw_ref, db_ref, lng_ref, lnb_ref, q_ref,
                      xp_ref, c_scr, sum_ref, mu_ref, rstd_ref):
    i = pl.program_id(0)
    ph = pl.program_id(1)
    c = pl.program_id(2)

    @pl.when(ph == 0)
    def _():
        xp_ref[HIST_B:, :] = g_ref[...]
        tail = g_ref[TM - HIST_B:, :]

        @pl.when(i == N_PROMPT_TILES)
        def _():
            xp_ref[:HIST_B, :] = st_ref[...]

        @pl.when((i < N_PROMPT_TILES) & (i % TILES_PER_SEQ == 0))
        def _():
            xp_ref[:HIST_B, :] = _stream_history(tail, jnp.zeros_like(tail))

        @pl.when((i < N_PROMPT_TILES) & (i % TILES_PER_SEQ != 0))
        def _():
            xp_ref[:HIST_B, :] = _stream_history(tail, prev_ref[prev_ref.shape[0] - HIST_B:, :])

        def body(r, carry):
            row0 = pl.multiple_of(r * CONV_ROWS, CONV_ROWS)
            for lc in range(TN // CONV_LANES):
                lanes = slice(lc * CONV_LANES, (lc + 1) * CONV_LANES)
                acc = jnp.broadcast_to(db_ref[:, lanes], (CONV_ROWS, CONV_LANES))
                for k in range(CONV_B):
                    acc = acc + dw_ref[k:k + 1, lanes] * xp_ref[pl.ds(row0 + k * NSTREAM, CONV_ROWS), lanes]
                c_scr[c, pl.ds(row0, CONV_ROWS), lanes] = acc
            return carry

        lax.fori_loop(0, TM // CONV_ROWS, body, 0)
        _accumulate_row_sum(c, sum_ref, c_scr[c])

    @pl.when((ph == 1) & (c == 0))
    def _():
        _row_stats(c_scr, sum_ref, mu_ref, rstd_ref)

    @pl.when(ph == 1)
    def _():
        y = (c_scr[c] - mu_ref[...]) * rstd_ref[...] * lng_ref[c] + lnb_ref[c]
        q_ref[...] = _silu(y).astype(q_ref.dtype)


def _conf_conv(g, state_tm, w_dw, b_dw, ln_g3, ln_b3):
    prev_rows = TM // 2

    def col(ph, c):
        return jnp.where(ph == 0, c, NCB - 1)

    return pl.pallas_call(
        _conf_conv_kernel,
        grid=(N_TILES, 2, NCB),
        in_specs=[pl.BlockSpec((TM, TN), lambda i, ph, c: (i, col(ph, c))),
                  pl.BlockSpec((prev_rows, TN), lambda i, ph, c: (jnp.maximum(2 * i - 1, 0), col(ph, c))),
                  pl.BlockSpec((HIST_B, TN), lambda i, ph, c: (0, col(ph, c))),
                  pl.BlockSpec((CONV_B, TN), lambda i, ph, c: (0, col(ph, c))),
                  pl.BlockSpec((1, TN), lambda i, ph, c: (0, col(ph, c))),
                  pl.BlockSpec((NCB, 1, TN), lambda i, ph, c: (0, 0, 0)),
                  pl.BlockSpec((NCB, 1, TN), lambda i, ph, c: (0, 0, 0))],
        out_specs=pl.BlockSpec((TM, TN), lambda i, ph, c: (i, c * ph)),
        out_shape=jax.ShapeDtypeStruct((ROWS, D_MODEL), BF16),
        scratch_shapes=[pltpu.VMEM((HIST_B + TM, TN), F32), pltpu.VMEM((NCB, TM, TN), F32),
                        pltpu.VMEM((TM, 1), F32), pltpu.VMEM((TM, 1), F32), pltpu.VMEM((TM, 1), F32)],
        compiler_params=_params(("arbitrary", "arbitrary", "arbitrary")),
        name="conformer_conv",
    )(g, g, state_tm, w_dw, b_dw, ln_g3, ln_b3)


def _blocked(vec):
    return vec.reshape(NCB, 1, TN)


def _to_tiles(x_prompt, x_sample):
    p = x_prompt.reshape(BATCH * TILES_PER_SEQ, NSTREAM, TSTEP, NCB, TN)
    p = p.transpose(3, 0, 2, 1, 4).reshape(NCB, N_PROMPT_TILES * TM, TN)
    s = x_sample.reshape(DEC_BATCH, DEC_SEQ, NCB, TN).transpose(2, 1, 0, 3).reshape(NCB, TM, TN)
    return jnp.concatenate([p, s], axis=1)


def _from_tiles(x3):
    p = x3[:, :N_PROMPT_TILES * TM].reshape(NCB, BATCH * TILES_PER_SEQ, TSTEP, NSTREAM, TN)
    p = p.transpose(1, 3, 2, 0, 4).reshape(BATCH, SEQ, D_MODEL)
    s = x3[:, N_PROMPT_TILES * TM:].reshape(NCB, DEC_SEQ, DEC_BATCH, TN).transpose(2, 1, 0, 3)
    return p, s.reshape(DEC_BATCH, DEC_SEQ, D_MODEL)


def _tile_tails(rows, steps):
    width = rows.shape[-1]
    tiles = rows.reshape(N_TILES, -1, NSTREAM, width)[:, -steps:]
    last = [TILES_PER_SEQ * (b + 1) - 1 for b in range(BATCH)]
    prompt = jnp.stack([tiles[t, :, NSTREAM - 1] for t in last])
    sample = tiles[N_PROMPT_TILES].transpose(1, 0, 2)
    return prompt, sample


def _state_time_major(state):
    return state.transpose(1, 0, 2).reshape(state.shape[1] * DEC_BATCH, state.shape[2])


def _mix_weights(w_s, b_s):
    pos = jnp.arange(MLP_CHUNK)
    half = MLP_CHUNK // 2
    mask = (pos[None, :] // half) <= (pos[:, None] // half)
    w_m = jnp.where(mask[None], w_s, jnp.zeros((), w_s.dtype))
    eye_p = jnp.eye(TM // MLP_CHUNK, dtype=w_s.dtype)
    eye_s = jnp.eye(DEC_BATCH, dtype=w_s.dtype)
    nat_p = jnp.einsum("ab,gij->gaibj", eye_p, w_m).reshape(G_A, TM, TM)
    nat_s = jnp.einsum("ab,gij->gaibj", eye_s, w_m[:, :DEC_SEQ, :DEC_SEQ]).reshape(G_A, TM, TM)
    r = jnp.arange(TM)
    perm = (r % NSTREAM) * TSTEP + r // NSTREAM
    wmix = jnp.stack([nat_p, nat_s])[:, :, perm][:, :, :, perm].astype(BF16)
    b_p = b_s[:, perm % MLP_CHUNK]
    b_smp = b_s[:, perm % TSTEP]
    bmix = jnp.stack([b_p, b_smp])[..., None]
    return wmix, bmix


def kernel(x_prompt, x_sample, state_conv, state_ffn, norm_mix_pre, norm_mix_post, norm_ffn_pre, norm_ffn_post,
           a_w_in, a_b_in, a_ln_g, a_ln_b, a_w_s, a_b_s, a_w_out,
           b_w_in, b_b_in, b_w_dw, b_b_dw, b_ln_g, b_ln_b, b_w_out,
           f_w_up, f_w_dw, f_b_dw, f_w_down):
    a_w_in, a_w_out, b_w_in, b_w_out, f_w_up, f_w_down = (
        w.astype(BF16) for w in (a_w_in, a_w_out, b_w_in, b_w_out, f_w_up, f_w_down))
    a_b_in, b_b_in, f_b_dw = (b[:, None, :] for b in (a_b_in, b_b_in, f_b_dw))
    ffn_state_tm = state_ffn.transpose(0, 2, 1, 3).reshape(DEPTH, HIST_F, state_ffn.shape[-1])

    x3 = _to_tiles(x_prompt, x_sample)
    h = _rms_first(x3, _blocked(norm_mix_pre[0]))

    conv_p, conv_s, ffn_p, ffn_s, v_s = [], [], [], [], []
    for layer in range(DEPTH):
        j = layer // 2
        post = _blocked(norm_mix_post[layer])
        ffn_pre = _blocked(norm_ffn_pre[layer])
        if layer % 2 == 0:
            u, v = _mm_in("gelu", h, a_w_in, j, (a_b_in,), "gmlp_in")
            wmix, bmix = _mix_weights(a_w_s[j], a_b_s[j])
            p, vn = _gmlp_mix(u, v, _blocked(a_ln_g[j]), _blocked(a_ln_b[j]), wmix, bmix)
            v_s.append(_tile_tails(vn, TSTEP)[1])
            x3, h = _mm_out(p, a_w_out, j, x3, post, ffn_pre, "gmlp_out")
        else:
            (g,) = _mm_in("glu", h, b_w_in, j, (b_b_in,), "conformer_in")
            q = _conf_conv(g, _state_time_major(state_conv[j]), b_w_dw[j], b_b_dw[j][None],
                           _blocked(b_ln_g[j]), _blocked(b_ln_b[j]))
            g_p, g_s = _tile_tails(g, CONV_B - 1)
            conv_p.append(g_p)
            conv_s.append(g_s)
            x3, h = _mm_out(q, b_w_out, j, x3, post, ffn_pre, "conformer_out")

        s, tail_a, tail_b = _mm_in("ffn", h, f_w_up, layer, (f_w_dw, f_b_dw, ffn_state_tm), "ffn_up")
        t_p, t_s = _tile_tails(jnp.concatenate([tail_a, tail_b], axis=-1), CONV_F - 1)
        ffn_p.append(t_p)
        ffn_s.append(t_s)
        next_pre = _blocked(norm_mix_pre[layer + 1]) if layer + 1 < DEPTH else None
        x3, h = _mm_out(s, f_w_down, layer, x3, _blocked(norm_ffn_post[layer]), next_pre, "ffn_down")

    y_prompt, y_sample = _from_tiles(x3)
    return (y_prompt, y_sample, jnp.stack(conv_p), jnp.stack(ffn_p), jnp.stack(conv_s), jnp.stack(ffn_s),
            jnp.stack(v_s))
```

```python
import functools

import jax
import jax.numpy as jnp
from jax import lax
from jax.experimental import pallas as pl
from jax.experimental.pallas import tpu as pltpu

D_MODEL = 4096
DEPTH = 4
SEQ = 4096
BATCH = 2
DEC_BATCH = 8
DEC_SEQ = 64
CONV_B = 31
CONV_F = 3
G_A = 8
MLP_CHUNK = 128
EPS = 1e-6

TM = 512
NSTREAM = 8
TSTEP = TM // NSTREAM
TN = 512
NCB = D_MODEL // TN
N_PROMPT_TILES = BATCH * SEQ // TM
N_TILES = N_PROMPT_TILES + 1
TILES_PER_SEQ = SEQ // TM
ROWS = N_TILES * TM
HIST_B = (CONV_B - 1) * NSTREAM
HIST_F = (CONV_F - 1) * NSTREAM
VMEM_LIMIT = 56 * 1024 * 1024
NORM_ROWS = 32

F32 = jnp.float32
BF16 = jnp.bfloat16


def _params(sem):
    return pltpu.CompilerParams(dimension_semantics=sem, vmem_limit_bytes=VMEM_LIMIT)


def _dot(a, b):
    return jnp.dot(a, b, preferred_element_type=F32)


def _gelu(x):
    return 0.5 * x * (1.0 + lax.erf(x * 0.7071067811865476))


def _silu(x):
    return x * jax.nn.sigmoid(x)


def _sublane_iota(shape):
    return lax.broadcasted_iota(jnp.int32, shape, 1)


def _stream_history(tail_cur, tail_prev):
    rows, n = tail_cur.shape
    cur = pltpu.roll(tail_cur.reshape(rows // NSTREAM, NSTREAM, n), 1, 1)
    prev = pltpu.roll(tail_prev.reshape(rows // NSTREAM, NSTREAM, n), 1, 1)
    hist = jnp.where(_sublane_iota(cur.shape) == 0, prev, cur)
    return hist.reshape(rows, n)


def _rms_first_kernel(x_ref, g_ref, h_ref):
    def body(r, carry):
        sl = pl.ds(pl.multiple_of(r * NORM_ROWS, NORM_ROWS), NORM_ROWS)
        ss = jnp.zeros((NORM_ROWS, 1), F32)
        for cb in range(NCB):
            x = x_ref[cb, sl, :]
            ss = ss + jnp.sum(x * x, axis=-1, keepdims=True)
        scale = lax.rsqrt(ss * (1.0 / D_MODEL) + EPS)
        for cb in range(NCB):
            h_ref[sl, cb * TN:(cb + 1) * TN] = (x_ref[cb, sl, :] * scale * g_ref[cb]).astype(BF16)
        return carry

    lax.fori_loop(0, TM // NORM_ROWS, body, 0)


def _rms_first(x3, gain3):
    return pl.pallas_call(
        _rms_first_kernel,
        grid=(N_TILES,),
        in_specs=[pl.BlockSpec((NCB, TM, TN), lambda i: (0, i, 0)),
                  pl.BlockSpec((NCB, 1, TN), lambda i: (0, 0, 0))],
        out_specs=pl.BlockSpec((TM, D_MODEL), lambda i: (i, 0)),
        out_shape=jax.ShapeDtypeStruct((ROWS, D_MODEL), BF16),
        compiler_params=_params(("arbitrary",)),
        name="rms_first",
    )(x3, gain3)


def _gelu_epilogue(rows, prev, raw, ba_ref, bb_ref, u_ref, v_ref):
    u_ref[rows, :] = _gelu(raw[0, rows, :] + ba_ref[...]).astype(u_ref.dtype)
    v_ref[rows, :] = _gelu(raw[1, rows, :] + bb_ref[...])


def _glu_epilogue(rows, prev, raw, ba_ref, bb_ref, o_ref):
    o_ref[rows, :] = (raw[0, rows, :] + ba_ref[...]) * jax.nn.sigmoid(raw[1, rows, :] + bb_ref[...])


def _ffn_history(prev, raw, dwa_ref, dwb_ref, dba_ref, dbb_ref, sta_ref, stb_ref, s_ref, taila_ref, tailb_ref,
                 hist_ref, carry_ref):
    ip, jp = prev
    for half, (st_ref, tail_ref) in enumerate(((sta_ref, taila_ref), (stb_ref, tailb_ref))):
        tail = raw[half, TM - HIST_F:, :]
        tail_ref[...] = tail

        @pl.when(ip % TILES_PER_SEQ == 0)
        def _():
            carry_ref[jp, half] = jnp.zeros((HIST_F, TN), F32)

        @pl.when(ip < N_PROMPT_TILES)
        def _():
            hist_ref[half] = _stream_history(tail, carry_ref[jp, half])

        @pl.when(ip == N_PROMPT_TILES)
        def _():
            hist_ref[half] = st_ref[...]

        carry_ref[jp, half] = tail


def _ffn_epilogue(rows, prev, raw, dwa_ref, dwb_ref, dba_ref, dbb_ref, sta_ref, stb_ref, s_ref, taila_ref,
                  tailb_ref, hist_ref, carry_ref):
    for r0 in range(rows.start, rows.stop, EPI_ROWS):
        conv = []
        for half, (dw_ref, db_ref) in enumerate(((dwa_ref, dba_ref), (dwb_ref, dbb_ref))):
            c = db_ref[...] + dw_ref[CONV_F - 1:CONV_F, :] * raw[half, r0:r0 + EPI_ROWS, :]
            for k in range(CONV_F - 1):
                back = (CONV_F - 1 - k) * NSTREAM
                if r0 == 0:
                    window = jnp.concatenate([hist_ref[half, HIST_F - back:, :], raw[half, :EPI_ROWS - back, :]],
                                             axis=0)
                else:
                    window = raw[half, r0 - back:r0 + EPI_ROWS - back, :]
                c = c + dw_ref[k:k + 1, :] * window
            conv.append(c)
        gate, val = conv
        s_ref[r0:r0 + EPI_ROWS, :] = (_silu(gate) * val).astype(s_ref.dtype)


DOT_ROWS = 256
EPI_ROWS = 32


def _mm_in_kernel(nh, n_epi_in, n_out, pre_fn, epi_fn, h_ref, wa_ref, wb_ref, *rest):
    epi_refs = rest[:n_epi_in + n_out]
    raw_ref = rest[n_epi_in + n_out]
    extra = rest[n_epi_in + n_out + 1:]
    s = pl.program_id(0)
    t = jnp.maximum(s - 1, 0)
    prev = (t // nh, t % nh)

    @pl.when(s == 0)
    def _():
        raw_ref[1] = jnp.zeros(raw_ref.shape[1:], F32)

    def step(raw_w, raw_r):
        if pre_fn is not None:
            pre_fn(prev, raw_r, *epi_refs, *extra)
        pieces = [(c, half) for c in range(TM // DOT_ROWS) for half in (0, 1)]
        chunk = TM // len(pieces)
        pending = None
        for k, (c, half) in enumerate(pieces):
            rows = slice(c * DOT_ROWS, (c + 1) * DOT_ROWS)
            r = _dot(h_ref[rows, :], (wa_ref, wb_ref)[half][...])
            if pending is not None:
                raw_w[pending[0], pending[1], :] = pending[2]
            pending = (half, rows, r)
            epi_fn(slice(k * chunk, (k + 1) * chunk), prev, raw_r, *epi_refs, *extra)
        raw_w[pending[0], pending[1], :] = pending[2]

    if pre_fn is not None:
        step(raw_ref.at[s % 2], raw_ref.at[(s + 1) % 2])
    else:
        @pl.when(s % 2 == 0)
        def _():
            step(raw_ref.at[0], raw_ref.at[1])

        @pl.when(s % 2 == 1)
        def _():
            step(raw_ref.at[1], raw_ref.at[0])


def _mm_in(kind, h, w, layer, epi_inputs, name):
    nh = w.shape[2] // 2 // TN
    steps = N_TILES * nh

    def cur(s):
        t = jnp.minimum(s, steps - 1)
        return t // nh, t % nh

    def prev(s):
        t = jnp.maximum(s - 1, 0)
        return t // nh, t % nh

    def half_specs(rows):
        a = pl.BlockSpec((None, rows, TN), lambda s: (layer, 0, prev(s)[1]))
        b = pl.BlockSpec((None, rows, TN), lambda s: (layer, 0, prev(s)[1] + nh))
        return [a, b]

    tile = pl.BlockSpec((TM, TN), lambda s: prev(s))
    in_specs = [pl.BlockSpec((TM, D_MODEL), lambda s: (cur(s)[0], 0)),
                pl.BlockSpec((None, D_MODEL, TN), lambda s: (layer, 0, cur(s)[1])),
                pl.BlockSpec((None, D_MODEL, TN), lambda s: (layer, 0, cur(s)[1] + nh))]
    scratch = [pltpu.VMEM((2, 2, TM, TN), F32)]
    cols = nh * TN
    if kind == "gelu":
        (bias,) = epi_inputs
        args = [bias, bias]
        in_specs += half_specs(1)
        out_specs = [tile, tile]
        out_shape = [jax.ShapeDtypeStruct((ROWS, cols), BF16), jax.ShapeDtypeStruct((ROWS, cols), F32)]
        pre_fn, epi_fn = None, _gelu_epilogue
    elif kind == "glu":
        (bias,) = epi_inputs
        args = [bias, bias]
        in_specs += half_specs(1)
        out_specs = [tile]
        out_shape = [jax.ShapeDtypeStruct((ROWS, cols), F32)]
        pre_fn, epi_fn = None, _glu_epilogue
    else:
        w_dw, b_dw, state_tm = epi_inputs
        args = [w_dw, w_dw, b_dw, b_dw, state_tm, state_tm]
        in_specs += half_specs(CONV_F) + half_specs(1) + half_specs(HIST_F)
        tail = pl.BlockSpec((HIST_F, TN), lambda s: prev(s))
        tail_shape = jax.ShapeDtypeStruct((N_TILES * HIST_F, cols), F32)
        out_specs = [tile, tail, tail]
        out_shape = [jax.ShapeDtypeStruct((ROWS, cols), BF16), tail_shape, tail_shape]
        scratch += [pltpu.VMEM((2, HIST_F, TN), F32), pltpu.VMEM((nh, 2, HIST_F, TN), F32)]
        pre_fn, epi_fn = _ffn_history, _ffn_epilogue
    return pl.pallas_call(
        functools.partial(_mm_in_kernel, nh, len(args), len(out_specs), pre_fn, epi_fn),
        grid=(steps + 1,),
        in_specs=in_specs,
        out_specs=out_specs,
        out_shape=out_shape,
        scratch_shapes=scratch,
        compiler_params=_params(("arbitrary",)),
        name=name,
    )(h, w, w, *args)


OUT_TN = 1024
OUT_COL_STEPS = D_MODEL // OUT_TN
OUT_COL_BLOCKS = OUT_TN // TN


def _mm_out_kernel(kpass, emit_h, lhs_ref, w_ref, x_ref, gpost_ref, gpre_ref, *rest):
    if emit_h:
        xo_ref, h_ref, m_ref = rest
    else:
        xo_ref, m_ref = rest
        h_ref = None
    i = pl.program_id(0)
    kk = pl.program_id(1)
    j = pl.program_id(2)
    pieces = [(c, ch) for c in range(TM // DOT_ROWS) for ch in range(OUT_COL_BLOCKS)]
    rows = TM // (kpass * OUT_COL_STEPS)
    sub = rows // len(pieces)
    q = kk * OUT_COL_STEPS + j
    m_cur = m_ref.at[i % 2]
    m_prev = m_ref.at[(i + 1) % 2]

    def norm_rows(k):
        sl = pl.ds(pl.multiple_of(q * rows + k * sub, sub), sub)
        xs = slice(k * sub, (k + 1) * sub)
        ss = jnp.zeros((sub, 1), F32)
        for cb in range(NCB):
            m = m_prev[cb, sl, :]
            ss = ss + jnp.sum(m * m, axis=-1, keepdims=True)
        scale = lax.rsqrt(ss * (1.0 / D_MODEL) + EPS)
        ss = jnp.zeros((sub, 1), F32)
        for cb in range(NCB):
            xn = x_ref[cb, xs, :] + m_prev[cb, sl, :] * scale * gpost_ref[cb]
            xo_ref[cb, xs, :] = xn
            ss = ss + jnp.sum(xn * xn, axis=-1, keepdims=True)
        if emit_h:
            scale = lax.rsqrt(ss * (1.0 / D_MODEL) + EPS)
            for cb in range(NCB):
                h_ref[xs, cb * TN:(cb + 1) * TN] = (xo_ref[cb, xs, :] * scale * gpre_ref[cb]).astype(BF16)

    def body(do_dot, accumulate, do_norm):
        def flush(cb, rws, r):
            m_cur[cb, rws, :] = m_cur[cb, rws, :] + r if accumulate else r

        pending = None
        for k, (c, ch) in enumerate(pieces):
            if do_dot:
                rws = slice(c * DOT_ROWS, (c + 1) * DOT_ROWS)
                r = _dot(lhs_ref[rws, :], w_ref[:, ch * TN:(ch + 1) * TN])
                if pending is not None:
                    flush(*pending)
                pending = (j * OUT_COL_BLOCKS + ch, rws, r)
            if do_norm:
                norm_rows(k)
        if pending is not None:
            flush(*pending)

    for accumulate in ((False,) if kpass == 1 else (False, True)):
        k_cond = (kk > 0) if accumulate else (kk == 0)

        @pl.when((i == 0) & k_cond)
        def _():
            body(True, accumulate, False)

        @pl.when((i >= 1) & (i < N_TILES) & k_cond)
        def _():
            body(True, accumulate, True)

    @pl.when(i == N_TILES)
    def _():
        body(False, False, True)


def _mm_out(lhs, w, layer, x3, gpost3, gpre3, name):
    kpass = w.shape[1] // D_MODEL
    nq = kpass * OUT_COL_STEPS
    rows = TM // nq
    emit_h = gpre3 is not None
    if not emit_h:
        gpre3 = gpost3

    def live(i, v, last):
        return jnp.where(i < N_TILES, v, last)

    def chunk(i, kk, j):
        return jnp.where(i >= 1, (i - 1) * nq + kk * OUT_COL_STEPS + j, 0)

    xblock = pl.BlockSpec((NCB, rows, TN), lambda i, kk, j: (0, chunk(i, kk, j), 0))
    gain = pl.BlockSpec((NCB, 1, TN), lambda i, kk, j: (0, 0, 0))
    out_specs = [xblock]
    out_shape = [jax.ShapeDtypeStruct((NCB, ROWS, TN), F32)]
    if emit_h:
        out_specs.append(pl.BlockSpec((rows, D_MODEL), lambda i, kk, j: (chunk(i, kk, j), 0)))
        out_shape.append(jax.ShapeDtypeStruct((ROWS, D_MODEL), BF16))
    outs = pl.pallas_call(
        functools.partial(_mm_out_kernel, kpass, emit_h),
        grid=(N_TILES + 1, kpass, OUT_COL_STEPS),
        in_specs=[pl.BlockSpec((TM, D_MODEL), lambda i, kk, j: (jnp.minimum(i, N_TILES - 1), live(i, kk, kpass - 1))),
                  pl.BlockSpec((None, D_MODEL, OUT_TN),
                               lambda i, kk, j: (layer, live(i, kk, kpass - 1), live(i, j, OUT_COL_STEPS - 1))),
                  xblock, gain, gain],
        out_specs=out_specs,
        out_shape=out_shape,
        scratch_shapes=[pltpu.VMEM((2, NCB, TM, TN), F32)],
        compiler_params=_params(("arbitrary", "arbitrary", "arbitrary")),
        name=name,
    )(lhs, w, x3, gpost3, gpre3)
    return outs if emit_h else (outs[0], None)


def _row_stats(src_ref, sum_ref, mu_ref, rstd_ref):
    mu = sum_ref[...] * (1.0 / D_MODEL)
    var = jnp.zeros((TM, 1), F32)
    for cb in range(NCB):
        d = src_ref[cb] - mu
        var = var + jnp.sum(d * d, axis=-1, keepdims=True)
    mu_ref[...] = mu
    rstd_ref[...] = lax.rsqrt(var * (1.0 / D_MODEL) + EPS)


def _accumulate_row_sum(c, sum_ref, val):
    @pl.when(c == 0)
    def _():
        sum_ref[...] = jnp.sum(val, axis=-1, keepdims=True)

    @pl.when(c > 0)
    def _():
        sum_ref[...] += jnp.sum(val, axis=-1, keepdims=True)


def _gmlp_mix_kernel(v_ref, u_ref, lng_ref, lnb_ref, wmix_ref, bmix_ref, p_ref, vn_ref,
                     v_scr, sum_ref, mu_ref, rstd_ref):
    ph = pl.program_id(1)
    c = pl.program_id(2)

    @pl.when(ph == 0)
    def _():
        v = v_ref[...]
        v_scr[c] = v
        _accumulate_row_sum(c, sum_ref, v)

    @pl.when((ph == 1) & (c == 0))
    def _():
        _row_stats(v_scr, sum_ref, mu_ref, rstd_ref)

    @pl.when(ph == 1)
    def _():
        vn = (v_scr[c] - mu_ref[...]) * rstd_ref[...] * lng_ref[c] + lnb_ref[c]
        vn_ref[...] = vn
        mixed = _dot(wmix_ref[0, 0], vn.astype(BF16)) + bmix_ref[0, 0]
        p_ref[...] = (u_ref[...].astype(F32) * mixed).astype(p_ref.dtype)


def _gmlp_mix(u, v, ln_g3, ln_b3, wmix, bmix):
    def variant(i):
        return jnp.where(i == N_PROMPT_TILES, 1, 0)

    return pl.pallas_call(
        _gmlp_mix_kernel,
        grid=(N_TILES, 2, NCB),
        in_specs=[pl.BlockSpec((TM, TN), lambda i, ph, c: (i, jnp.where(ph == 0, c, NCB - 1))),
                  pl.BlockSpec((TM, TN), lambda i, ph, c: (i, c * ph)),
                  pl.BlockSpec((NCB, 1, TN), lambda i, ph, c: (0, 0, 0)),
                  pl.BlockSpec((NCB, 1, TN), lambda i, ph, c: (0, 0, 0)),
                  pl.BlockSpec((1, 1, TM, TM), lambda i, ph, c: (variant(i), c * ph, 0, 0)),
                  pl.BlockSpec((1, 1, TM, 1), lambda i, ph, c: (variant(i), c * ph, 0, 0))],
        out_specs=[pl.BlockSpec((TM, TN), lambda i, ph, c: (i, c * ph)),
                   pl.BlockSpec((TM, TN), lambda i, ph, c: (i, c * ph))],
        out_shape=[jax.ShapeDtypeStruct((ROWS, D_MODEL), BF16), jax.ShapeDtypeStruct((ROWS, D_MODEL), F32)],
        scratch_shapes=[pltpu.VMEM((NCB, TM, TN), F32), pltpu.VMEM((TM, 1), F32),
                        pltpu.VMEM((TM, 1), F32), pltpu.VMEM((TM, 1), F32)],
        compiler_params=_params(("arbitrary", "arbitrary", "arbitrary")),
        name="gmlp_mix",
    )(v, u, ln_g3, ln_b3, wmix, bmix)


CONV_ROWS = 64
CONV_LANES = 128


def _conf_conv_kernel(g_ref, prev_ref, st_ref, dw_ref, db_ref, lng_ref, lnb_ref, q_ref,
                      xp_ref, c_scr, sum_ref, mu_ref, rstd_ref):
    i = pl.program_id(0)
    ph = pl.program_id(1)
    c = pl.program_id(2)

    @pl.when(ph == 0)
    def _():
        xp_ref[HIST_B:, :] = g_ref[...]
        tail = g_ref[TM - HIST_B:, :]

        @pl.when(i == N_PROMPT_TILES)
        def _():
            xp_ref[:HIST_B, :] = st_ref[...]

        @pl.when((i < N_PROMPT_TILES) & (i % TILES_PER_SEQ == 0))
        def _():
            xp_ref[:HIST_B, :] = _stream_history(tail, jnp.zeros_like(tail))

        @pl.when((i < N_PROMPT_TILES) & (i % TILES_PER_SEQ != 0))
        def _():
            xp_ref[:HIST_B, :] = _stream_history(tail, prev_ref[prev_ref.shape[0] - HIST_B:, :])

        for lc in range(TN // CONV_LANES):
            lanes = slice(lc * CONV_LANES, (lc + 1) * CONV_LANES)
            taps = [dw_ref[k:k + 1, lanes] for k in range(CONV_B)]
            bias = db_ref[:, lanes]
            for row0 in range(0, TM, CONV_ROWS):
                acc = jnp.broadcast_to(bias, (CONV_ROWS, CONV_LANES))
                for k in range(CONV_B):
                    acc = acc + taps[k] * xp_ref[row0 + k * NSTREAM:row0 + k * NSTREAM + CONV_ROWS, lanes]
                c_scr[c, row0:row0 + CONV_ROWS, lanes] = acc
        _accumulate_row_sum(c, sum_ref, c_scr[c])

    @pl.when((ph == 1) & (c == 0))
    def _():
        _row_stats(c_scr, sum_ref, mu_ref, rstd_ref)

    @pl.when(ph == 1)
    def _():
        y = (c_scr[c] - mu_ref[...]) * rstd_ref[...] * lng_ref[c] + lnb_ref[c]
        q_ref[...] = _silu(y).astype(q_ref.dtype)


def _conf_conv(g, state_tm, w_dw, b_dw, ln_g3, ln_b3):
    prev_rows = TM // 2

    def col(ph, c):
        return jnp.where(ph == 0, c, NCB - 1)

    return pl.pallas_call(
        _conf_conv_kernel,
        grid=(N_TILES, 2, NCB),
        in_specs=[pl.BlockSpec((TM, TN), lambda i, ph, c: (i, col(ph, c))),
                  pl.BlockSpec((prev_rows, TN), lambda i, ph, c: (jnp.maximum(2 * i - 1, 0), col(ph, c))),
                  pl.BlockSpec((HIST_B, TN), lambda i, ph, c: (0, col(ph, c))),
                  pl.BlockSpec((CONV_B, TN), lambda i, ph, c: (0, col(ph, c))),
                  pl.BlockSpec((1, TN), lambda i, ph, c: (0, col(ph, c))),
                  pl.BlockSpec((NCB, 1, TN), lambda i, ph, c: (0, 0, 0)),
                  pl.BlockSpec((NCB, 1, TN), lambda i, ph, c: (0, 0, 0))],
        out_specs=pl.BlockSpec((TM, TN), lambda i, ph, c: (i, c * ph)),
        out_shape=jax.ShapeDtypeStruct((ROWS, D_MODEL), BF16),
        scratch_shapes=[pltpu.VMEM((HIST_B + TM, TN), F32), pltpu.VMEM((NCB, TM, TN), F32),
                        pltpu.VMEM((TM, 1), F32), pltpu.VMEM((TM, 1), F32), pltpu.VMEM((TM, 1), F32)],
        compiler_params=_params(("arbitrary", "arbitrary", "arbitrary")),
        name="conformer_conv",
    )(g, g, state_tm, w_dw, b_dw, ln_g3, ln_b3)


def _blocked(vec):
    return vec.reshape(NCB, 1, TN)


def _to_tiles(x_prompt, x_sample):
    p = x_prompt.reshape(BATCH * TILES_PER_SEQ, NSTREAM, TSTEP, NCB, TN)
    p = p.transpose(3, 0, 2, 1, 4).reshape(NCB, N_PROMPT_TILES * TM, TN)
    s = x_sample.reshape(DEC_BATCH, DEC_SEQ, NCB, TN).transpose(2, 1, 0, 3).reshape(NCB, TM, TN)
    return jnp.concatenate([p, s], axis=1)


def _from_tiles(x3):
    p = x3[:, :N_PROMPT_TILES * TM].reshape(NCB, BATCH * TILES_PER_SEQ, TSTEP, NSTREAM, TN)
    p = p.transpose(1, 3, 2, 0, 4).reshape(BATCH, SEQ, D_MODEL)
    s = x3[:, N_PROMPT_TILES * TM:].reshape(NCB, DEC_SEQ, DEC_BATCH, TN).transpose(2, 1, 0, 3)
    return p, s.reshape(DEC_BATCH, DEC_SEQ, D_MODEL)


def _tile_tails(rows, steps):
    width = rows.shape[-1]
    tiles = rows.reshape(N_TILES, -1, NSTREAM, width)[:, -steps:]
    last = [TILES_PER_SEQ * (b + 1) - 1 for b in range(BATCH)]
    prompt = jnp.stack([tiles[t, :, NSTREAM - 1] for t in last])
    sample = tiles[N_PROMPT_TILES].transpose(1, 0, 2)
    return prompt, sample


def _state_time_major(state):
    return state.transpose(1, 0, 2).reshape(state.shape[1] * DEC_BATCH, state.shape[2])


def _mix_weights(w_s, b_s):
    pos = jnp.arange(MLP_CHUNK)
    half = MLP_CHUNK // 2
    mask = (pos[None, :] // half) <= (pos[:, None] // half)
    w_m = jnp.where(mask[None], w_s, jnp.zeros((), w_s.dtype))
    eye_p = jnp.eye(TM // MLP_CHUNK, dtype=w_s.dtype)
    eye_s = jnp.eye(DEC_BATCH, dtype=w_s.dtype)
    nat_p = jnp.einsum("ab,gij->gaibj", eye_p, w_m).reshape(G_A, TM, TM)
    nat_s = jnp.einsum("ab,gij->gaibj", eye_s, w_m[:, :DEC_SEQ, :DEC_SEQ]).reshape(G_A, TM, TM)
    r = jnp.arange(TM)
    perm = (r % NSTREAM) * TSTEP + r // NSTREAM
    wmix = jnp.stack([nat_p, nat_s])[:, :, perm][:, :, :, perm].astype(BF16)
    b_p = b_s[:, perm % MLP_CHUNK]
    b_smp = b_s[:, perm % TSTEP]
    bmix = jnp.stack([b_p, b_smp])[..., None]
    return wmix, bmix


def kernel(x_prompt, x_sample, state_conv, state_ffn, norm_mix_pre, norm_mix_post, norm_ffn_pre, norm_ffn_post,
           a_w_in, a_b_in, a_ln_g, a_ln_b, a_w_s, a_b_s, a_w_out,
           b_w_in, b_b_in, b_w_dw, b_b_dw, b_ln_g, b_ln_b, b_w_out,
           f_w_up, f_w_dw, f_b_dw, f_w_down):
    a_w_in, a_w_out, b_w_in, b_w_out, f_w_up, f_w_down = (
        w.astype(BF16) for w in (a_w_in, a_w_out, b_w_in, b_w_out, f_w_up, f_w_down))
    a_b_in, b_b_in, f_b_dw = (b[:, None, :] for b in (a_b_in, b_b_in, f_b_dw))
    ffn_state_tm = state_ffn.transpose(0, 2, 1, 3).reshape(DEPTH, HIST_F, state_ffn.shape[-1])

    x3 = _to_tiles(x_prompt, x_sample)
    h = _rms_first(x3, _blocked(norm_mix_pre[0]))

    conv_p, conv_s, ffn_p, ffn_s, v_s = [], [], [], [], []
    for layer in range(DEPTH):
        j = layer // 2
        post = _blocked(norm_mix_post[layer])
        ffn_pre = _blocked(norm_ffn_pre[layer])
        if layer % 2 == 0:
            u, v = _mm_in("gelu", h, a_w_in, j, (a_b_in,), "gmlp_in")
            wmix, bmix = _mix_weights(a_w_s[j], a_b_s[j])
            p, vn = _gmlp_mix(u, v, _blocked(a_ln_g[j]), _blocked(a_ln_b[j]), wmix, bmix)
            v_s.append(_tile_tails(vn, TSTEP)[1])
            x3, h = _mm_out(p, a_w_out, j, x3, post, ffn_pre, "gmlp_out")
        else:
            (g,) = _mm_in("glu", h, b_w_in, j, (b_b_in,), "conformer_in")
            q = _conf_conv(g, _state_time_major(state_conv[j]), b_w_dw[j], b_b_dw[j][None],
                           _blocked(b_ln_g[j]), _blocked(b_ln_b[j]))
            g_p, g_s = _tile_tails(g, CONV_B - 1)
            conv_p.append(g_p)
            conv_s.append(g_s)
            x3, h = _mm_out(q, b_w_out, j, x3, post, ffn_pre, "conformer_out")

        s, tail_a, tail_b = _mm_in("ffn", h, f_w_up, layer, (f_w_dw, f_b_dw, ffn_state_tm), "ffn_up")
        t_p, t_s = _tile_tails(jnp.concatenate([tail_a, tail_b], axis=-1), CONV_F - 1)
        ffn_p.append(t_p)
        ffn_s.append(t_s)
        next_pre = _blocked(norm_mix_pre[layer + 1]) if layer + 1 < DEPTH else None
        x3, h = _mm_out(s, f_w_down, layer, x3, _blocked(norm_ffn_post[layer]), next_pre, "ffn_down")

    y_prompt, y_sample = _from_tiles(x3)
    return (y_prompt, y_sample, jnp.stack(conv_p), jnp.stack(ffn_p), jnp.stack(conv_s), jnp.stack(ffn_s),
            jnp.stack(v_s))
```

```python
import functools

import jax
import jax.numpy as jnp
from jax import lax
from jax.experimental import pallas as pl
from jax.experimental.pallas import tpu as pltpu

D_MODEL = 4096
DEPTH = 4
SEQ = 4096
BATCH = 2
DEC_BATCH = 8
DEC_SEQ = 64
CONV_B = 31
CONV_F = 3
G_A = 8
MLP_CHUNK = 128
EPS = 1e-6

TM = 512
NSTREAM = 8
TSTEP = TM // NSTREAM
TN = 512
NCB = D_MODEL // TN
N_PROMPT_TILES = BATCH * SEQ // TM
N_TILES = N_PROMPT_TILES + 1
TILES_PER_SEQ = SEQ // TM
ROWS = N_TILES * TM
HIST_B = (CONV_B - 1) * NSTREAM
HIST_F = (CONV_F - 1) * NSTREAM
VMEM_LIMIT = 56 * 1024 * 1024
NORM_ROWS = 32

F32 = jnp.float32
BF16 = jnp.bfloat16


def _params(sem):
    return pltpu.CompilerParams(dimension_semantics=sem, vmem_limit_bytes=VMEM_LIMIT)


def _dot(a, b):
    return jnp.dot(a, b, preferred_element_type=F32)


def _gelu(x):
    return 0.5 * x * (1.0 + lax.erf(x * 0.7071067811865476))


def _silu(x):
    return x * jax.nn.sigmoid(x)


def _sublane_iota(shape):
    return lax.broadcasted_iota(jnp.int32, shape, 1)


def _stream_history(tail_cur, tail_prev):
    rows, n = tail_cur.shape
    cur = pltpu.roll(tail_cur.reshape(rows // NSTREAM, NSTREAM, n), 1, 1)
    prev = pltpu.roll(tail_prev.reshape(rows // NSTREAM, NSTREAM, n), 1, 1)
    hist = jnp.where(_sublane_iota(cur.shape) == 0, prev, cur)
    return hist.reshape(rows, n)


def _rms_first_kernel(x_ref, g_ref, h_ref):
    def body(r, carry):
        sl = pl.ds(pl.multiple_of(r * NORM_ROWS, NORM_ROWS), NORM_ROWS)
        ss = jnp.zeros((NORM_ROWS, 1), F32)
        for cb in range(NCB):
            x = x_ref[cb, sl, :]
            ss = ss + jnp.sum(x * x, axis=-1, keepdims=True)
        scale = lax.rsqrt(ss * (1.0 / D_MODEL) + EPS)
        for cb in range(NCB):
            h_ref[sl, cb * TN:(cb + 1) * TN] = (x_ref[cb, sl, :] * scale * g_ref[cb]).astype(BF16)
        return carry

    lax.fori_loop(0, TM // NORM_ROWS, body, 0)


def _rms_first(x3, gain3):
    return pl.pallas_call(
        _rms_first_kernel,
        grid=(N_TILES,),
        in_specs=[pl.BlockSpec((NCB, TM, TN), lambda i: (0, i, 0)),
                  pl.BlockSpec((NCB, 1, TN), lambda i: (0, 0, 0))],
        out_specs=pl.BlockSpec((TM, D_MODEL), lambda i: (i, 0)),
        out_shape=jax.ShapeDtypeStruct((ROWS, D_MODEL), BF16),
        compiler_params=_params(("arbitrary",)),
        name="rms_first",
    )(x3, gain3)


def _gelu_epilogue(rows, prev, raw, ba_ref, bb_ref, u_ref, v_ref):
    u_ref[rows, :] = _gelu(raw[0, rows, :] + ba_ref[...]).astype(u_ref.dtype)
    v_ref[rows, :] = _gelu(raw[1, rows, :] + bb_ref[...])


def _glu_epilogue(rows, prev, raw, ba_ref, bb_ref, o_ref):
    o_ref[rows, :] = (raw[0, rows, :] + ba_ref[...]) * jax.nn.sigmoid(raw[1, rows, :] + bb_ref[...])


def _ffn_history(prev, raw, dwa_ref, dwb_ref, dba_ref, dbb_ref, sta_ref, stb_ref, s_ref, taila_ref, tailb_ref,
                 hist_ref, carry_ref):
    ip, jp = prev
    for half, (st_ref, tail_ref) in enumerate(((sta_ref, taila_ref), (stb_ref, tailb_ref))):
        tail = raw[half, TM - HIST_F:, :]
        tail_ref[...] = tail

        @pl.when(ip % TILES_PER_SEQ == 0)
        def _():
            carry_ref[jp, half] = jnp.zeros((HIST_F, TN), F32)

        @pl.when(ip < N_PROMPT_TILES)
        def _():
            hist_ref[half] = _stream_history(tail, carry_ref[jp, half])

        @pl.when(ip == N_PROMPT_TILES)
        def _():
            hist_ref[half] = st_ref[...]

        carry_ref[jp, half] = tail


def _ffn_epilogue(rows, prev, raw, dwa_ref, dwb_ref, dba_ref, dbb_ref, sta_ref, stb_ref, s_ref, taila_ref,
                  tailb_ref, hist_ref, carry_ref):
    for r0 in range(rows.start, rows.stop, EPI_ROWS):
        conv = []
        for half, (dw_ref, db_ref) in enumerate(((dwa_ref, dba_ref), (dwb_ref, dbb_ref))):
            c = db_ref[...] + dw_ref[CONV_F - 1:CONV_F, :] * raw[half, r0:r0 + EPI_ROWS, :]
            for k in range(CONV_F - 1):
                back = (CONV_F - 1 - k) * NSTREAM
                if r0 == 0:
                    window = jnp.concatenate([hist_ref[half, HIST_F - back:, :], raw[half, :EPI_ROWS - back, :]],
                                             axis=0)
                else:
                    window = raw[half, r0 - back:r0 + EPI_ROWS - back, :]
                c = c + dw_ref[k:k + 1, :] * window
            conv.append(c)
        gate, val = conv
        s_ref[r0:r0 + EPI_ROWS, :] = (_silu(gate) * val).astype(s_ref.dtype)


DOT_ROWS = 256
EPI_ROWS = 32


def _mm_in_kernel(nh, n_epi_in, n_out, pre_fn, epi_fn, h_ref, wa_ref, wb_ref, *rest):
    epi_refs = rest[:n_epi_in + n_out]
    raw_ref = rest[n_epi_in + n_out]
    extra = rest[n_epi_in + n_out + 1:]
    s = pl.program_id(0)
    t = jnp.maximum(s - 1, 0)
    prev = (t // nh, t % nh)

    @pl.when(s == 0)
    def _():
        raw_ref[1] = jnp.zeros(raw_ref.shape[1:], F32)

    def step(raw_w, raw_r):
        if pre_fn is not None:
            pre_fn(prev, raw_r, *epi_refs, *extra)
        pieces = [(c, half) for c in range(TM // DOT_ROWS) for half in (0, 1)]
        chunk = TM // len(pieces)
        pending = None
        for k, (c, half) in enumerate(pieces):
            rows = slice(c * DOT_ROWS, (c + 1) * DOT_ROWS)
            r = _dot(h_ref[rows, :], (wa_ref, wb_ref)[half][...])
            if pending is not None:
                raw_w[pending[0], pending[1], :] = pending[2]
            pending = (half, rows, r)
            epi_fn(slice(k * chunk, (k + 1) * chunk), prev, raw_r, *epi_refs, *extra)
        raw_w[pending[0], pending[1], :] = pending[2]

    if epi_fn is not _glu_epilogue:
        step(raw_ref.at[s % 2], raw_ref.at[(s + 1) % 2])
    else:
        @pl.when(s % 2 == 0)
        def _():
            step(raw_ref.at[0], raw_ref.at[1])

        @pl.when(s % 2 == 1)
        def _():
            step(raw_ref.at[1], raw_ref.at[0])


def _mm_in(kind, h, w, layer, epi_inputs, name):
    nh = w.shape[2] // 2 // TN
    steps = N_TILES * nh

    def cur(s):
        t = jnp.minimum(s, steps - 1)
        return t // nh, t % nh

    def prev(s):
        t = jnp.maximum(s - 1, 0)
        return t // nh, t % nh

    def half_specs(rows):
        a = pl.BlockSpec((None, rows, TN), lambda s: (layer, 0, prev(s)[1]))
        b = pl.BlockSpec((None, rows, TN), lambda s: (layer, 0, prev(s)[1] + nh))
        return [a, b]

    tile = pl.BlockSpec((TM, TN), lambda s: prev(s))
    in_specs = [pl.BlockSpec((TM, D_MODEL), lambda s: (cur(s)[0], 0)),
                pl.BlockSpec((None, D_MODEL, TN), lambda s: (layer, 0, cur(s)[1])),
                pl.BlockSpec((None, D_MODEL, TN), lambda s: (layer, 0, cur(s)[1] + nh))]
    scratch = [pltpu.VMEM((2, 2, TM, TN), F32)]
    cols = nh * TN
    if kind == "gelu":
        (bias,) = epi_inputs
        args = [bias, bias]
        in_specs += half_specs(1)
        out_specs = [tile, tile]
        out_shape = [jax.ShapeDtypeStruct((ROWS, cols), BF16), jax.ShapeDtypeStruct((ROWS, cols), F32)]
        pre_fn, epi_fn = None, _gelu_epilogue
    elif kind == "glu":
        (bias,) = epi_inputs
        args = [bias, bias]
        in_specs += half_specs(1)
        out_specs = [tile]
        out_shape = [jax.ShapeDtypeStruct((ROWS, cols), F32)]
        pre_fn, epi_fn = None, _glu_epilogue
    else:
        w_dw, b_dw, state_tm = epi_inputs
        args = [w_dw, w_dw, b_dw, b_dw, state_tm, state_tm]
        in_specs += half_specs(CONV_F) + half_specs(1) + half_specs(HIST_F)
        tail = pl.BlockSpec((HIST_F, TN), lambda s: prev(s))
        tail_shape = jax.ShapeDtypeStruct((N_TILES * HIST_F, cols), F32)
        out_specs = [tile, tail, tail]
        out_shape = [jax.ShapeDtypeStruct((ROWS, cols), BF16), tail_shape, tail_shape]
        scratch += [pltpu.VMEM((2, HIST_F, TN), F32), pltpu.VMEM((nh, 2, HIST_F, TN), F32)]
        pre_fn, epi_fn = _ffn_history, _ffn_epilogue
    return pl.pallas_call(
        functools.partial(_mm_in_kernel, nh, len(args), len(out_specs), pre_fn, epi_fn),
        grid=(steps + 1,),
        in_specs=in_specs,
        out_specs=out_specs,
        out_shape=out_shape,
        scratch_shapes=scratch,
        compiler_params=_params(("arbitrary",)),
        name=name,
    )(h, w, w, *args)


OUT_TN = 1024
OUT_COL_STEPS = D_MODEL // OUT_TN
OUT_COL_BLOCKS = OUT_TN // TN


def _mm_out_kernel(kpass, emit_h, lhs_ref, w_ref, x_ref, gpost_ref, gpre_ref, *rest):
    if emit_h:
        xo_ref, h_ref, m_ref = rest
    else:
        xo_ref, m_ref = rest
        h_ref = None
    i = pl.program_id(0)
    kk = pl.program_id(1)
    j = pl.program_id(2)
    pieces = [(c, ch) for c in range(TM // DOT_ROWS) for ch in range(OUT_COL_BLOCKS)]
    rows = TM // (kpass * OUT_COL_STEPS)
    sub = rows // len(pieces)
    q = kk * OUT_COL_STEPS + j
    m_cur = m_ref.at[i % 2]
    m_prev = m_ref.at[(i + 1) % 2]

    def norm_rows(k):
        sl = pl.ds(pl.multiple_of(q * rows + k * sub, sub), sub)
        xs = slice(k * sub, (k + 1) * sub)
        ss = jnp.zeros((sub, 1), F32)
        for cb in range(NCB):
            m = m_prev[cb, sl, :]
            ss = ss + jnp.sum(m * m, axis=-1, keepdims=True)
        scale = lax.rsqrt(ss * (1.0 / D_MODEL) + EPS)
        ss = jnp.zeros((sub, 1), F32)
        for cb in range(NCB):
            xn = x_ref[cb, xs, :] + m_prev[cb, sl, :] * scale * gpost_ref[cb]
            xo_ref[cb, xs, :] = xn
            ss = ss + jnp.sum(xn * xn, axis=-1, keepdims=True)
        if emit_h:
            scale = lax.rsqrt(ss * (1.0 / D_MODEL) + EPS)
            for cb in range(NCB):
                h_ref[xs, cb * TN:(cb + 1) * TN] = (xo_ref[cb, xs, :] * scale * gpre_ref[cb]).astype(BF16)

    def body(do_dot, do_norm):
        def flush(cb, rws, r):
            if kpass > 1:
                r = r + m_cur[cb, rws, :] * (kk > 0).astype(F32)
            m_cur[cb, rws, :] = r

        pending = None
        for k, (c, ch) in enumerate(pieces):
            if do_dot:
                rws = slice(c * DOT_ROWS, (c + 1) * DOT_ROWS)
                r = _dot(lhs_ref[rws, :], w_ref[:, ch * TN:(ch + 1) * TN])
                if pending is not None:
                    flush(*pending)
                pending = (j * OUT_COL_BLOCKS + ch, rws, r)
            if do_norm:
                norm_rows(k)
        if pending is not None:
            flush(*pending)

    @pl.when((i == 0) & (kk == 0) & (j == 0))
    def _():
        def zero(n, carry):
            m_ref[n // NCB, n % NCB] = jnp.zeros((TM, TN), F32)
            return carry

        lax.fori_loop(0, 2 * NCB, zero, 0)

    @pl.when(i < N_TILES)
    def _():
        body(True, True)

    @pl.when(i == N_TILES)
    def _():
        body(False, True)


def _mm_out(lhs, w, layer, x3, gpost3, gpre3, name):
    kpass = w.shape[1] // D_MODEL
    nq = kpass * OUT_COL_STEPS
    rows = TM // nq
    emit_h = gpre3 is not None
    if not emit_h:
        gpre3 = gpost3

    def live(i, v, last):
        return jnp.where(i < N_TILES, v, last)

    def chunk(i, kk, j):
        return jnp.where(i >= 1, (i - 1) * nq + kk * OUT_COL_STEPS + j, 0)

    xblock = pl.BlockSpec((NCB, rows, TN), lambda i, kk, j: (0, chunk(i, kk, j), 0))
    gain = pl.BlockSpec((NCB, 1, TN), lambda i, kk, j: (0, 0, 0))
    out_specs = [xblock]
    out_shape = [jax.ShapeDtypeStruct((NCB, ROWS, TN), F32)]
    if emit_h:
        out_specs.append(pl.BlockSpec((rows, D_MODEL), lambda i, kk, j: (chunk(i, kk, j), 0)))
        out_shape.append(jax.ShapeDtypeStruct((ROWS, D_MODEL), BF16))
    outs = pl.pallas_call(
        functools.partial(_mm_out_kernel, kpass, emit_h),
        grid=(N_TILES + 1, kpass, OUT_COL_STEPS),
        in_specs=[pl.BlockSpec((TM, D_MODEL), lambda i, kk, j: (jnp.minimum(i, N_TILES - 1), live(i, kk, kpass - 1))),
                  pl.BlockSpec((None, D_MODEL, OUT_TN),
                               lambda i, kk, j: (layer, live(i, kk, kpass - 1), live(i, j, OUT_COL_STEPS - 1))),
                  xblock, gain, gain],
        out_specs=out_specs,
        out_shape=out_shape,
        scratch_shapes=[pltpu.VMEM((2, NCB, TM, TN), F32)],
        compiler_params=_params(("arbitrary", "arbitrary", "arbitrary")),
        name=name,
    )(lhs, w, x3, gpost3, gpre3)
    return outs if emit_h else (outs[0], None)


def _row_stats(src_ref, sum_ref, mu_ref, rstd_ref):
    mu = sum_ref[...] * (1.0 / D_MODEL)
    var = jnp.zeros((TM, 1), F32)
    for cb in range(NCB):
        d = src_ref[cb] - mu
        var = var + jnp.sum(d * d, axis=-1, keepdims=True)
    mu_ref[...] = mu
    rstd_ref[...] = lax.rsqrt(var * (1.0 / D_MODEL) + EPS)


def _accumulate_row_sum(c, sum_ref, val):
    @pl.when(c == 0)
    def _():
        sum_ref[...] = jnp.sum(val, axis=-1, keepdims=True)

    @pl.when(c > 0)
    def _():
        sum_ref[...] += jnp.sum(val, axis=-1, keepdims=True)


def _gmlp_mix_kernel(v_ref, u_ref, lng_ref, lnb_ref, wmix_ref, bmix_ref, p_ref, vn_ref,
                     v_scr, sum_ref, mu_ref, rstd_ref):
    ph = pl.program_id(1)
    c = pl.program_id(2)

    @pl.when(ph == 0)
    def _():
        v = v_ref[...]
        v_scr[c] = v
        _accumulate_row_sum(c, sum_ref, v)

    @pl.when((ph == 1) & (c == 0))
    def _():
        _row_stats(v_scr, sum_ref, mu_ref, rstd_ref)

    @pl.when(ph == 1)
    def _():
        vn = (v_scr[c] - mu_ref[...]) * rstd_ref[...] * lng_ref[c] + lnb_ref[c]
        vn_ref[...] = vn
        mixed = _dot(wmix_ref[0, 0], vn.astype(BF16)) + bmix_ref[0, 0]
        p_ref[...] = (u_ref[...].astype(F32) * mixed).astype(p_ref.dtype)


def _gmlp_mix(u, v, ln_g3, ln_b3, wmix, bmix):
    def variant(i):
        return jnp.where(i == N_PROMPT_TILES, 1, 0)

    return pl.pallas_call(
        _gmlp_mix_kernel,
        grid=(N_TILES, 2, NCB),
        in_specs=[pl.BlockSpec((TM, TN), lambda i, ph, c: (i, jnp.where(ph == 0, c, NCB - 1))),
                  pl.BlockSpec((TM, TN), lambda i, ph, c: (i, c * ph)),
                  pl.BlockSpec((NCB, 1, TN), lambda i, ph, c: (0, 0, 0)),
                  pl.BlockSpec((NCB, 1, TN), lambda i, ph, c: (0, 0, 0)),
                  pl.BlockSpec((1, 1, TM, TM), lambda i, ph, c: (variant(i), c * ph, 0, 0)),
                  pl.BlockSpec((1, 1, TM, 1), lambda i, ph, c: (variant(i), c * ph, 0, 0))],
        out_specs=[pl.BlockSpec((TM, TN), lambda i, ph, c: (i, c * ph)),
                   pl.BlockSpec((TM, TN), lambda i, ph, c: (i, c * ph))],
        out_shape=[jax.ShapeDtypeStruct((ROWS, D_MODEL), BF16), jax.ShapeDtypeStruct((ROWS, D_MODEL), F32)],
        scratch_shapes=[pltpu.VMEM((NCB, TM, TN), F32), pltpu.VMEM((TM, 1), F32),
                        pltpu.VMEM((TM, 1), F32), pltpu.VMEM((TM, 1), F32)],
        compiler_params=_params(("arbitrary", "arbitrary", "arbitrary")),
        name="gmlp_mix",
    )(v, u, ln_g3, ln_b3, wmix, bmix)


CONV_ROWS = 64
CONV_LANES = 128


def _conf_conv_kernel(g_ref, prev_ref, st_ref, dw_ref, db_ref, lng_ref, lnb_ref, q_ref,
                      xp_ref, c_scr, sum_ref, mu_ref, rstd_ref):
    i = pl.program_id(0)
    ph = pl.program_id(1)
    c = pl.program_id(2)

    @pl.when(ph == 0)
    def _():
        xp_ref[HIST_B:, :] = g_ref[...]
        tail = g_ref[TM - HIST_B:, :]

        @pl.when(i == N_PROMPT_TILES)
        def _():
            xp_ref[:HIST_B, :] = st_ref[...]

        @pl.when((i < N_PROMPT_TILES) & (i % TILES_PER_SEQ == 0))
        def _():
            xp_ref[:HIST_B, :] = _stream_history(tail, jnp.zeros_like(tail))

        @pl.when((i < N_PROMPT_TILES) & (i % TILES_PER_SEQ != 0))
        def _():
            xp_ref[:HIST_B, :] = _stream_history(tail, prev_ref[prev_ref.shape[0] - HIST_B:, :])

        for lc in range(TN // CONV_LANES):
            lanes = slice(lc * CONV_LANES, (lc + 1) * CONV_LANES)
            taps = [dw_ref[k:k + 1, lanes] for k in range(CONV_B)]
            bias = db_ref[:, lanes]
            for row0 in range(0, TM, CONV_ROWS):
                acc = jnp.broadcast_to(bias, (CONV_ROWS, CONV_LANES))
                for k in range(CONV_B):
                    acc = acc + taps[k] * xp_ref[row0 + k * NSTREAM:row0 + k * NSTREAM + CONV_ROWS, lanes]
                c_scr[c, row0:row0 + CONV_ROWS, lanes] = acc
        _accumulate_row_sum(c, sum_ref, c_scr[c])

    @pl.when((ph == 1) & (c == 0))
    def _():
        _row_stats(c_scr, sum_ref, mu_ref, rstd_ref)

    @pl.when(ph == 1)
    def _():
        y = (c_scr[c] - mu_ref[...]) * rstd_ref[...] * lng_ref[c] + lnb_ref[c]
        q_ref[...] = _silu(y).astype(q_ref.dtype)


def _conf_conv(g, state_tm, w_dw, b_dw, ln_g3, ln_b3):
    prev_rows = TM // 2

    def col(ph, c):
        return jnp.where(ph == 0, c, NCB - 1)

    return pl.pallas_call(
        _conf_conv_kernel,
        grid=(N_TILES, 2, NCB),
        in_specs=[pl.BlockSpec((TM, TN), lambda i, ph, c: (i, col(ph, c))),
                  pl.BlockSpec((prev_rows, TN), lambda i, ph, c: (jnp.maximum(2 * i - 1, 0), col(ph, c))),
                  pl.BlockSpec((HIST_B, TN), lambda i, ph, c: (0, col(ph, c))),
                  pl.BlockSpec((CONV_B, TN), lambda i, ph, c: (0, col(ph, c))),
                  pl.BlockSpec((1, TN), lambda i, ph, c: (0, col(ph, c))),
                  pl.BlockSpec((NCB, 1, TN), lambda i, ph, c: (0, 0, 0)),
                  pl.BlockSpec((NCB, 1, TN), lambda i, ph, c: (0, 0, 0))],
        out_specs=pl.BlockSpec((TM, TN), lambda i, ph, c: (i, c * ph)),
        out_shape=jax.ShapeDtypeStruct((ROWS, D_MODEL), BF16),
        scratch_shapes=[pltpu.VMEM((HIST_B + TM, TN), F32), pltpu.VMEM((NCB, TM, TN), F32),
                        pltpu.VMEM((TM, 1), F32), pltpu.VMEM((TM, 1), F32), pltpu.VMEM((TM, 1), F32)],
        compiler_params=_params(("arbitrary", "arbitrary", "arbitrary")),
        name="conformer_conv",
    )(g, g, state_tm, w_dw, b_dw, ln_g3, ln_b3)


def _blocked(vec):
    return vec.reshape(NCB, 1, TN)


def _to_tiles(x_prompt, x_sample):
    p = x_prompt.reshape(BATCH * TILES_PER_SEQ, NSTREAM, TSTEP, NCB, TN)
    p = p.transpose(3, 0, 2, 1, 4).reshape(NCB, N_PROMPT_TILES * TM, TN)
    s = x_sample.reshape(DEC_BATCH, DEC_SEQ, NCB, TN).transpose(2, 1, 0, 3).reshape(NCB, TM, TN)
    return jnp.concatenate([p, s], axis=1)


def _from_tiles(x3):
    p = x3[:, :N_PROMPT_TILES * TM].reshape(NCB, BATCH * TILES_PER_SEQ, TSTEP, NSTREAM, TN)
    p = p.transpose(1, 3, 2, 0, 4).reshape(BATCH, SEQ, D_MODEL)
    s = x3[:, N_PROMPT_TILES * TM:].reshape(NCB, DEC_SEQ, DEC_BATCH, TN).transpose(2, 1, 0, 3)
    return p, s.reshape(DEC_BATCH, DEC_SEQ, D_MODEL)


def _tile_tails(rows, steps):
    width = rows.shape[-1]
    tiles = rows.reshape(N_TILES, -1, NSTREAM, width)[:, -steps:]
    last = [TILES_PER_SEQ * (b + 1) - 1 for b in range(BATCH)]
    prompt = jnp.stack([tiles[t, :, NSTREAM - 1] for t in last])
    sample = tiles[N_PROMPT_TILES].transpose(1, 0, 2)
    return prompt, sample


def _state_time_major(state):
    return state.transpose(1, 0, 2).reshape(state.shape[1] * DEC_BATCH, state.shape[2])


def _mix_weights(w_s, b_s):
    pos = jnp.arange(MLP_CHUNK)
    half = MLP_CHUNK // 2
    mask = (pos[None, :] // half) <= (pos[:, None] // half)
    w_m = jnp.where(mask[None], w_s, jnp.zeros((), w_s.dtype))
    eye_p = jnp.eye(TM // MLP_CHUNK, dtype=w_s.dtype)
    eye_s = jnp.eye(DEC_BATCH, dtype=w_s.dtype)
    nat_p = jnp.einsum("ab,gij->gaibj", eye_p, w_m).reshape(G_A, TM, TM)
    nat_s = jnp.einsum("ab,gij->gaibj", eye_s, w_m[:, :DEC_SEQ, :DEC_SEQ]).reshape(G_A, TM, TM)
    r = jnp.arange(TM)
    perm = (r % NSTREAM) * TSTEP + r // NSTREAM
    wmix = jnp.stack([nat_p, nat_s])[:, :, perm][:, :, :, perm].astype(BF16)
    b_p = b_s[:, perm % MLP_CHUNK]
    b_smp = b_s[:, perm % TSTEP]
    bmix = jnp.stack([b_p, b_smp])[..., None]
    return wmix, bmix


def kernel(x_prompt, x_sample, state_conv, state_ffn, norm_mix_pre, norm_mix_post, norm_ffn_pre, norm_ffn_post,
           a_w_in, a_b_in, a_ln_g, a_ln_b, a_w_s, a_b_s, a_w_out,
           b_w_in, b_b_in, b_w_dw, b_b_dw, b_ln_g, b_ln_b, b_w_out,
           f_w_up, f_w_dw, f_b_dw, f_w_down):
    a_w_in, a_w_out, b_w_in, b_w_out, f_w_up, f_w_down = (
        w.astype(BF16) for w in (a_w_in, a_w_out, b_w_in, b_w_out, f_w_up, f_w_down))
    a_b_in, b_b_in, f_b_dw = (b[:, None, :] for b in (a_b_in, b_b_in, f_b_dw))
    ffn_state_tm = state_ffn.transpose(0, 2, 1, 3).reshape(DEPTH, HIST_F, state_ffn.shape[-1])

    x3 = _to_tiles(x_prompt, x_sample)
    h = _rms_first(x3, _blocked(norm_mix_pre[0]))

    conv_p, conv_s, ffn_p, ffn_s, v_s = [], [], [], [], []
    for layer in range(DEPTH):
        j = layer // 2
        post = _blocked(norm_mix_post[layer])
        ffn_pre = _blocked(norm_ffn_pre[layer])
        if layer % 2 == 0:
            u, v = _mm_in("gelu", h, a_w_in, j, (a_b_in,), "gmlp_in")
            wmix, bmix = _mix_weights(a_w_s[j], a_b_s[j])
            p, vn = _gmlp_mix(u, v, _blocked(a_ln_g[j]), _blocked(a_ln_b[j]), wmix, bmix)
            v_s.append(_tile_tails(vn, TSTEP)[1])
            x3, h = _mm_out(p, a_w_out, j, x3, post, ffn_pre, "gmlp_out")
        else:
            (g,) = _mm_in("glu", h, b_w_in, j, (b_b_in,), "conformer_in")
            q = _conf_conv(g, _state_time_major(state_conv[j]), b_w_dw[j], b_b_dw[j][None],
                           _blocked(b_ln_g[j]), _blocked(b_ln_b[j]))
            g_p, g_s = _tile_tails(g, CONV_B - 1)
            conv_p.append(g_p)
            conv_s.append(g_s)
            x3, h = _mm_out(q, b_w_out, j, x3, post, ffn_pre, "conformer_out")

        s, tail_a, tail_b = _mm_in("ffn", h, f_w_up, layer, (f_w_dw, f_b_dw, ffn_state_tm), "ffn_up")
        t_p, t_s = _tile_tails(jnp.concatenate([tail_a, tail_b], axis=-1), CONV_F - 1)
        ffn_p.append(t_p)
        ffn_s.append(t_s)
        next_pre = _blocked(norm_mix_pre[layer + 1]) if layer + 1 < DEPTH else None
        x3, h = _mm_out(s, f_w_down, layer, x3, _blocked(norm_ffn_post[layer]), next_pre, "ffn_down")

    y_prompt, y_sample = _from_tiles(x3)
    return (y_prompt, y_sample, jnp.stack(conv_p), jnp.stack(ffn_p), jnp.stack(conv_s), jnp.stack(ffn_s),
            jnp.stack(v_s))
```
